```python
import jax, jax.numpy as jnp
from jax import lax
import numpy as np

D_MODEL = 4096
BATCH = 4
SEQ = 2048
DEPTH = 2
DEC_BATCH = 128
DEC_SEQ = 1
PAST_LEN = 16384
PAGE_SIZE = 128

N_EVEN = (DEPTH + 1) // 2
N_ODD = DEPTH // 2
MIX_WIDTH = D_MODEL
HALF = MIX_WIDTH // 2
A_DK = 128
A_HEADS = HALF // A_DK
A_DV = HALF // A_HEADS
B_CHUNK = 128
B_GROUPS = 16
B_GDIM = HALF // B_GROUPS
C_HEADS = 8
C_DK = HALF // C_HEADS
C_DV = HALF // C_HEADS
ROPE_BASE = 10000.0
D_HEADS = 4
D_QK = HALF // 2
D_DK = D_QK // D_HEADS
D_DV = HALF // D_HEADS
D_GATE_RANK = 16
D_GATE_TAU = 16.0
SCAN_CHUNK = 64
IN_AB = 6 * HALF
IN_CD = 4 * HALF + 2 * D_QK + 2 * HALF + D_GATE_RANK
N_GROUPS = 4
EXPERTS_PER_GROUP = 8
N_EXPERTS = N_GROUPS * EXPERTS_PER_GROUP
TOP_K = 2
D_EXPERT = 512
EPS = 1e-6

kernel_name = 'hybrid_hgrn2_gmlp_retention_gla_hmoe_step'


def rmsnorm(x, g):
    xf = x.astype(jnp.float32)
    r = lax.rsqrt(jnp.mean(xf * xf, axis=-1, keepdims=True) + EPS)
    return (xf * r * g.astype(jnp.float32)).astype(x.dtype)


def layernorm(x, g, b):
    xf = x.astype(jnp.float32)
    mu = jnp.mean(xf, axis=-1, keepdims=True)
    var = jnp.mean(jnp.square(xf - mu), axis=-1, keepdims=True)
    return ((xf - mu) * lax.rsqrt(var + EPS) * g.astype(jnp.float32) + b.astype(jnp.float32)).astype(x.dtype)


def head_rms(o, g):
    return o * lax.rsqrt(jnp.mean(o * o, axis=-1, keepdims=True) + EPS) * g.astype(jnp.float32)


def head_groupnorm(o, g):
    mu = jnp.mean(o, axis=-1, keepdims=True)
    var = jnp.mean(jnp.square(o - mu), axis=-1, keepdims=True)
    return (o - mu) * lax.rsqrt(var + EPS) * g.astype(jnp.float32)


def rotary(t, pos):
    half = t.shape[-1] // 2
    inv = ROPE_BASE ** (-jnp.arange(half, dtype=jnp.float32) / half)
    ang = pos.astype(jnp.float32)[:, None] * inv[None, :]
    cos = jnp.cos(ang)[None, :, None, :]
    sin = jnp.sin(ang)[None, :, None, :]
    tf = t.astype(jnp.float32)
    t1, t2 = tf[..., :half], tf[..., half:]
    return jnp.concatenate([t1 * cos - t2 * sin, t2 * cos + t1 * sin], axis=-1)


def gated_linear_chunked(q, k, v, g, s0):
    f32 = jnp.float32
    bsz, L, H, _ = q.shape
    dv = v.shape[-1]
    dg = g.shape[-1]
    c = min(SCAN_CHUNK, L)
    n = -(-L // c)
    pad = n * c - L

    def to_chunks(t):
        t = jnp.pad(t.astype(f32), ((0, 0), (0, pad), (0, 0), (0, 0)))
        return t.reshape(bsz, n, c, H, t.shape[-1]).transpose(1, 0, 3, 2, 4)

    qc, kc, vc, gc = to_chunks(q), to_chunks(k), to_chunks(v), to_chunks(g)
    causal = jnp.tril(jnp.ones((c, c), dtype=bool))

    def step(s, blk):
        qb, kb, vb, gb = blk
        G = jnp.cumsum(gb, axis=-2)
        G_last = G[..., -1:, :]
        o = jnp.einsum('bhik,bhkv->bhiv', qb * jnp.exp(G), s)
        if dg == 1:
            diff = G[..., :, None, 0] - G[..., None, :, 0]
            decay = jnp.exp(jnp.where(causal, diff, -jnp.inf))
            att = jnp.einsum('bhik,bhjk->bhij', qb, kb) * decay
        else:
            diff = G[..., :, None, :] - G[..., None, :, :]
            decay = jnp.exp(jnp.where(causal[:, :, None], diff, -jnp.inf))
            att = jnp.einsum('bhijk,bhjk->bhij', decay * qb[..., :, None, :], kb)
        o = o + jnp.einsum('bhij,bhjv->bhiv', att, vb)
        s = s * jnp.exp(jnp.swapaxes(G_last, -1, -2)) + jnp.einsum('bhjk,bhjv->bhkv', kb * jnp.exp(G_last - G), vb)
        return s, o

    s, o = lax.scan(step, s0.astype(f32), (qc, kc, vc, gc))
    o = o.transpose(1, 0, 3, 2, 4).reshape(bsz, n * c, H, dv)[:, :L]
    return o, s


def chunk_spatial_gate(u, v, w_s, b_s):
    bsz, L, W = v.shape
    n = -(-L // B_CHUNK)
    Lp = n * B_CHUNK
    vp = jnp.pad(v, ((0, 0), (0, Lp - L), (0, 0))).reshape(bsz, n, B_CHUNK, B_GROUPS, B_GDIM)
    wm = w_s * jnp.tril(jnp.ones((B_CHUNK, B_CHUNK), w_s.dtype))
    mixed = jnp.einsum('gij,bnjgd->bnigd', wm, vp) + b_s.T[None, None, :, :, None]
    mixed = mixed.reshape(bsz, Lp, W)[:, :L].astype(u.dtype)
    start = ((L - 1) // B_CHUNK) * B_CHUNK
    return u * mixed, v[:, start:]


def mixer_ab(h, s0, lb, w_in, w_out, hgrn_g, ln_g, ln_b, w_s, b_s):
    bsz, L, _ = h.shape
    f32 = jnp.float32
    z = h @ w_in
    q, f, inp, og, u, v = jnp.split(z, [HALF, 2 * HALF, 3 * HALF, 4 * HALF, 5 * HALF], axis=-1)
    ff = f.astype(f32)
    forget = lb + (1.0 - lb) * jax.nn.sigmoid(ff)
    log_f = jnp.log(forget)
    k = (1.0 - lb) * jax.nn.sigmoid(-ff)
    o_a, s_a = gated_linear_chunked(
        q.reshape(bsz, L, A_HEADS, A_DK) * (A_DK ** -0.5),
        k.reshape(bsz, L, A_HEADS, A_DK),
        inp.reshape(bsz, L, A_HEADS, A_DV),
        log_f.reshape(bsz, L, A_HEADS, A_DK), s0)
    o_a = head_rms(o_a, hgrn_g).reshape(bsz, L, HALF) * jax.nn.sigmoid(og.astype(f32))
    u = jax.nn.gelu(u, approximate=False)
    v = layernorm(jax.nn.gelu(v, approximate=False), ln_g, ln_b)
    o_b, v_rows = chunk_spatial_gate(u, v, w_s, b_s)
    y = jnp.concatenate([o_a.astype(h.dtype), o_b.astype(h.dtype)], axis=-1) @ w_out
    return y, s_a, v_rows


def mixer_cd(h, s_ret0, s_gla0, pos, w_in, w_out, ret_g, w_a2, b_a, gla_g):
    bsz, L, _ = h.shape
    f32 = jnp.float32
    z = h @ w_in
    b0 = 4 * HALF
    qc, kc, vc, gc, qd, kd, vd, rd, ad = jnp.split(
        z, [HALF, 2 * HALF, 3 * HALF, b0, b0 + D_QK, b0 + 2 * D_QK, b0 + 2 * D_QK + HALF, b0 + 2 * D_QK + 2 * HALF], axis=-1)
    log_gamma = jnp.log1p(-jnp.exp2(-5.0 - jnp.arange(C_HEADS, dtype=f32)))
    qr = rotary(qc.reshape(bsz, L, C_HEADS, C_DK), pos)
    kr = rotary(kc.reshape(bsz, L, C_HEADS, C_DK), pos) * (C_DK ** -0.5)
    g_c = jnp.broadcast_to(log_gamma[None, None, :, None], (bsz, L, C_HEADS, 1))
    o_c, s_c = gated_linear_chunked(qr, kr, vc.reshape(bsz, L, C_HEADS, C_DV), g_c, s_ret0)
    o_c = head_groupnorm(o_c, ret_g.reshape(C_HEADS, C_DV)).reshape(bsz, L, HALF) * jax.nn.silu(gc.astype(f32))
    log_a = jax.nn.log_sigmoid((ad @ w_a2 + b_a).astype(f32)) / D_GATE_TAU
    o_d, s_d = gated_linear_chunked(
        qd.reshape(bsz, L, D_HEADS, D_DK) * (D_DK ** -0.5),
        kd.reshape(bsz, L, D_HEADS, D_DK),
        vd.reshape(bsz, L, D_HEADS, D_DV),
        log_a.reshape(bsz, L, D_HEADS, D_DK), s_gla0)
    o_d = head_rms(o_d, gla_g).reshape(bsz, L, HALF) * jax.nn.silu(rd.astype(f32))
    y = jnp.concatenate([o_c.astype(h.dtype), o_d.astype(h.dtype)], axis=-1) @ w_out
    return y, s_c, s_d


def hier_moe(h, wg, bg, we, be, w1, w3, w2):
    f32 = jnp.float32
    shp = h.shape
    t = h.reshape(-1, shp[-1])
    lg = (t @ wg).astype(f32) + bg.astype(f32)
    p_group = jnp.max(jax.nn.softmax(lg, axis=-1), axis=-1, keepdims=True)
    g_idx = jnp.argmax(lg, axis=-1)
    le = jnp.einsum('td,gde->tge', t, we).astype(f32) + be.astype(f32)
    le_sel = jnp.sum(le * jax.nn.one_hot(g_idx, N_GROUPS, dtype=f32)[:, :, None], axis=1)
    e_val, e_idx = lax.top_k(le_sel, TOP_K)
    w_top = jax.nn.softmax(e_val, axis=-1) * p_group
    expert = g_idx[:, None] * EXPERTS_PER_GROUP + e_idx
    gates = jnp.sum(jax.nn.one_hot(expert, N_EXPERTS, dtype=f32) * w_top[..., None], axis=1)
    a = jnp.einsum('td,edf->tef', t, w1)
    b = jnp.einsum('td,edf->tef', t, w3)
    hid = jax.nn.silu(a) * b * gates[..., None].astype(t.dtype)
    return jnp.einsum('tef,efd->td', hid, w2).reshape(shp)


def setup_inputs(seed: int = 0) -> dict:
    key = jax.random.key(seed)
    ks = iter(jax.random.split(key, 40))

    def nrm(shape, scale):
        return jax.random.normal(next(ks), shape, jnp.float32) * scale

    return {
        'x_prompt': nrm((BATCH, SEQ, D_MODEL), 1.0),
        'x_sample': nrm((DEC_BATCH, DEC_SEQ, D_MODEL), 1.0),
        'state_hgrn': nrm((N_EVEN, DEC_BATCH, A_HEADS, A_DK, A_DV), 0.5),
        'state_ret': nrm((N_ODD, DEC_BATCH, C_HEADS, C_DK, C_DV), 0.5),
        'state_gla': nrm((N_ODD, DEC_BATCH, D_HEADS, D_DK, D_DV), 0.5),
        'ln_mix': 1.0 + nrm((DEPTH, D_MODEL), 0.02),
        'ln_ffn': 1.0 + nrm((DEPTH, D_MODEL), 0.02),
        'ln_final': 1.0 + nrm((D_MODEL,), 0.02),
        'w_in_ab': nrm((N_EVEN, D_MODEL, IN_AB), D_MODEL ** -0.5),
        'w_out_ab': nrm((N_EVEN, MIX_WIDTH, D_MODEL), MIX_WIDTH ** -0.5),
        'lb_logits': nrm((N_EVEN + 1, HALF), 0.5),
        'hgrn_norm': 1.0 + nrm((N_EVEN, A_DV), 0.02),
        'gmlp_ln_g': 1.0 + nrm((N_EVEN, HALF), 0.02),
        'gmlp_ln_b': nrm((N_EVEN, HALF), 0.02),
        'gmlp_w_s': nrm((N_EVEN, B_GROUPS, B_CHUNK, B_CHUNK), B_CHUNK ** -0.5),
        'gmlp_b_s': nrm((N_EVEN, B_GROUPS, B_CHUNK), 0.02),
        'w_in_cd': nrm((N_ODD, D_MODEL, IN_CD), D_MODEL ** -0.5),
        'w_out_cd': nrm((N_ODD, MIX_WIDTH, D_MODEL), MIX_WIDTH ** -0.5),
        'ret_norm': 1.0 + nrm((N_ODD, HALF), 0.02),
        'gla_w_a2': nrm((N_ODD, D_GATE_RANK, D_QK), D_GATE_RANK ** -0.5),
        'gla_b_a': nrm((N_ODD, D_QK), 0.02),
        'gla_norm': 1.0 + nrm((N_ODD, D_DV), 0.02),
        'router_group_w': nrm((DEPTH, D_MODEL, N_GROUPS), D_MODEL ** -0.5),
        'router_group_b': nrm((DEPTH, N_GROUPS), 0.01),
        'router_expert_w': nrm((DEPTH, N_GROUPS, D_MODEL, EXPERTS_PER_GROUP), D_MODEL ** -0.5),
        'router_expert_b': nrm((DEPTH, N_GROUPS, EXPERTS_PER_GROUP), 0.01),
        'moe_w1': nrm((DEPTH, N_EXPERTS, D_MODEL, D_EXPERT), D_MODEL ** -0.5),
        'moe_w3': nrm((DEPTH, N_EXPERTS, D_MODEL, D_EXPERT), D_MODEL ** -0.5),
        'moe_w2': nrm((DEPTH, N_EXPERTS, D_EXPERT, D_MODEL), D_EXPERT ** -0.5),
    }


def reference(x_prompt, x_sample, state_hgrn, state_ret, state_gla, ln_mix, ln_ffn, ln_final,
              w_in_ab, w_out_ab, lb_logits, hgrn_norm, gmlp_ln_g, gmlp_ln_b, gmlp_w_s, gmlp_b_s,
              w_in_cd, w_out_cd, ret_norm, gla_w_a2, gla_b_a, gla_norm,
              router_group_w, router_group_b, router_expert_w, router_expert_b, moe_w1, moe_w3, moe_w2):
    f32 = jnp.float32
    bp, bs = x_prompt.shape[0], x_sample.shape[0]
    pos_p = jnp.arange(x_prompt.shape[1], dtype=jnp.int32)
    pos_s = PAST_LEN + jnp.arange(x_sample.shape[1], dtype=jnp.int32)
    lower_bounds = jnp.cumsum(jax.nn.softmax(lb_logits.astype(f32), axis=0), axis=0)

    xp, xs = x_prompt, x_sample
    hg_p, hg_s, cv_p, cv_s, rt_p, rt_s, gl_p, gl_s = [], [], [], [], [], [], [], []
    for l in range(DEPTH):
        j = l // 2
        hp = rmsnorm(xp, ln_mix[l])
        hs = rmsnorm(xs, ln_mix[l])
        if l % 2 == 0:
            wts = (lower_bounds[j], w_in_ab[j], w_out_ab[j], hgrn_norm[j], gmlp_ln_g[j], gmlp_ln_b[j], gmlp_w_s[j], gmlp_b_s[j])
            yp, sap, vp = mixer_ab(hp, jnp.zeros((bp, A_HEADS, A_DK, A_DV), f32), *wts)
            ys, sas, vs = mixer_ab(hs, state_hgrn[j], *wts)
            hg_p.append(sap.astype(x_prompt.dtype))
            hg_s.append(sas.astype(x_sample.dtype))
            cv_p.append(vp)
            cv_s.append(vs)
        else:
            wts = (w_in_cd[j], w_out_cd[j], ret_norm[j], gla_w_a2[j], gla_b_a[j], gla_norm[j])
            yp, scp, sdp = mixer_cd(hp, jnp.zeros((bp, C_HEADS, C_DK, C_DV), f32),
                                    jnp.zeros((bp, D_HEADS, D_DK, D_DV), f32), pos_p, *wts)
            ys, scs, sds = mixer_cd(hs, state_ret[j], state_gla[j], pos_s, *wts)
            rt_p.append(scp.astype(x_prompt.dtype))
            rt_s.append(scs.astype(x_sample.dtype))
            gl_p.append(sdp.astype(x_prompt.dtype))
            gl_s.append(sds.astype(x_sample.dtype))
        xp = xp + yp
        xs = xs + ys
        mw = (router_group_w[l], router_group_b[l], router_expert_w[l], router_expert_b[l], moe_w1[l], moe_w3[l], moe_w2[l])
        xp = xp + hier_moe(rmsnorm(xp, ln_ffn[l]), *mw)
        xs = xs + hier_moe(rmsnorm(xs, ln_ffn[l]), *mw)

    y_prompt = rmsnorm(xp, ln_final)
    y_sample = rmsnorm(xs, ln_final)
    hgrn_prompt = jnp.stack(hg_p)
    hgrn_sample = jnp.stack(hg_s)
    chunkv_prompt = jnp.stack(cv_p)
    chunkv_sample = jnp.stack(cv_s)
    ret_prompt = jnp.stack(rt_p)
    ret_sample = jnp.stack(rt_s)
    gla_prompt = jnp.stack(gl_p)
    gla_sample = jnp.stack(gl_s)
    return (y_prompt, y_sample, hgrn_prompt, hgrn_sample, chunkv_prompt, chunkv_sample, ret_prompt, ret_sample, gla_prompt, gla_sample)
```

```python
import functools

import jax
import jax.numpy as jnp
from jax import lax
from jax.experimental import pallas as pl
from jax.experimental.pallas import tpu as pltpu

F32 = jnp.float32
BF16 = jnp.bfloat16

D_MODEL = 4096
HALF = D_MODEL // 2
PAST_LEN = 16384
A_DK, A_DV, A_HEADS = 128, 128, 16
B_CHUNK, B_GROUPS, B_GDIM = 128, 16, 128
C_HEADS, C_DK, C_DV = 8, 256, 256
ROPE_BASE = 10000.0
D_HEADS, D_QK, D_DK, D_DV = 4, 1024, 256, 512
D_GATE_RANK, D_GATE_TAU = 16, 16.0
IN_AB = 6 * HALF
IN_CD_MAIN = 4 * HALF + 2 * D_QK + 2 * HALF
N_GROUPS, EXPERTS_PER_GROUP, N_EXPERTS, TOP_K, D_EXPERT = 4, 8, 32, 2, 512
EPS = 1e-6

LANES = 128
SUBLANES = 8
BF16_ROWS = 16
REC_CHUNK = 32
RET_CHUNK = 256
SAMPLE_BLOCK = 16
MOE_TILE = 256

_TN = (((0,), (0,)), ((), ()))
_NT = (((1,), (1,)), ((), ()))


def _cp(sem, vmem_mb):
    return pltpu.CompilerParams(dimension_semantics=sem, vmem_limit_bytes=vmem_mb << 20)


def _pick(n, candidates):
    for c in candidates:
        if n % c == 0:
            return c
    raise ValueError(f"no block size for {n}")


def _tile_lanes(x, rep):
    return x if rep == 1 else jnp.concatenate([x] * rep, axis=1)


def _split3(x):
    hi = x.astype(BF16).astype(F32)
    r = x - hi
    mid = r.astype(BF16).astype(F32)
    return hi, mid, r - mid


def _gelu(x):
    return 0.5 * x * (1.0 + lax.erf(x * 0.7071067811865476))


def _silu(x):
    return x * jax.nn.sigmoid(x)


def _norm_kernel(x_ref, g_ref, o_ref):
    x = x_ref[...]
    r = lax.rsqrt(jnp.mean(x * x, axis=-1, keepdims=True) + EPS)
    o_ref[...] = (x * r * g_ref[...]).astype(o_ref.dtype)


def _rmsnorm(x, g, out_dtype):
    t, d = x.shape
    tm = _pick(t, (320, 256, 128, 64, 32, 16))
    return pl.pallas_call(
        _norm_kernel, grid=(t // tm,),
        in_specs=[pl.BlockSpec((tm, d), lambda i: (i, 0)), pl.BlockSpec((1, d), lambda i: (0, 0))],
        out_specs=pl.BlockSpec((tm, d), lambda i: (i, 0)),
        out_shape=jax.ShapeDtypeStruct((t, d), out_dtype),
        compiler_params=_cp(("parallel",), 40), name="rmsnorm",
    )(x, g.reshape(1, d))


def _norm_router_kernel(x_ref, g_ref, whi_ref, wlo_ref, h_ref, lg_ref):
    x = x_ref[...]
    r = lax.rsqrt(jnp.mean(x * x, axis=-1, keepdims=True) + EPS)
    hf = x * r * g_ref[...]
    h = hf.astype(BF16)
    h_ref[...] = h
    lo = (hf - h.astype(F32)).astype(BF16)
    whi = whi_ref[...]
    lg_ref[...] = (jnp.dot(h, whi, preferred_element_type=F32) + jnp.dot(lo, whi, preferred_element_type=F32)
                   + jnp.dot(h, wlo_ref[...], preferred_element_type=F32))


def _rmsnorm_router(x, g, wr):
    t, d = x.shape
    tm = _pick(t, (320, 256, 128, 64, 32, 16))
    whi = wr.astype(BF16)
    wlo = (wr - whi.astype(F32)).astype(BF16)
    return pl.pallas_call(
        _norm_router_kernel, grid=(t // tm,),
        in_specs=[pl.BlockSpec((tm, d), lambda i: (i, 0)), pl.BlockSpec((1, d), lambda i: (0, 0)),
                  pl.BlockSpec((d, LANES), lambda i: (0, 0)), pl.BlockSpec((d, LANES), lambda i: (0, 0))],
        out_specs=[pl.BlockSpec((tm, d), lambda i: (i, 0)), pl.BlockSpec((tm, LANES), lambda i: (i, 0))],
        out_shape=[jax.ShapeDtypeStruct((t, d), BF16), jax.ShapeDtypeStruct((t, LANES), F32)],
        compiler_params=_cp(("parallel",), 40), name="rmsnorm_router",
    )(x, g.reshape(1, d), whi, wlo)


def _mm_kernel(h_ref, w_ref, o_ref):
    o_ref[...] = jnp.dot(h_ref[...], w_ref[...].astype(BF16), preferred_element_type=F32)


def _mm_res_kernel(h_ref, w_ref, r_ref, o_ref):
    o_ref[...] = r_ref[...] + jnp.dot(h_ref[...], w_ref[...].astype(BF16), preferred_element_type=F32)


def _matmul(h, w, widx, n_out, res=None, out_rows=None):
    t, k = h.shape
    tm = _pick(t, (2048, 1664, 1280, 1024, 640, 512, 256, 128))
    tn = _pick(n_out, (256, 128) if tm > 1664 else (512, 256, 128))
    in_specs = [pl.BlockSpec((tm, k), lambda i, j: (i, 0), pipeline_mode=pl.Buffered(1)),
                pl.BlockSpec((None, k, tn), lambda i, j: (widx, 0, j))]
    args = [h, w]
    kern = _mm_kernel
    if res is not None:
        in_specs.append(pl.BlockSpec((tm, tn), lambda i, j: (i, j)))
        args.append(res)
        kern = _mm_res_kernel
    return pl.pallas_call(
        kern, grid=(t // tm, n_out // tn), in_specs=in_specs,
        out_specs=pl.BlockSpec((tm, tn), lambda i, j: (i, j)),
        out_shape=jax.ShapeDtypeStruct((out_rows or t, n_out), F32),
        compiler_params=_cp(("parallel", "arbitrary"), 56), name="proj",
    )(*args)


def _hp_mm_kernel(*refs, norm, has_res, m):
    x_ref, g_ref, w_ref = refs[:3]
    r_ref = refs[3] if has_res else None
    o_ref, hl_scr = refs[-2], refs[-1]

    @pl.when(pl.program_id(0) == 0)
    def _():
        x = x_ref[...]
        if norm:
            x = x * lax.rsqrt(jnp.mean(x * x, axis=-1, keepdims=True) + EPS) * g_ref[...]
        hi = x.astype(BF16)
        hl_scr[:m, :] = hi
        hl_scr[m:, :] = (x - hi.astype(F32)).astype(BF16)

    w = w_ref[...]
    w_hi = w.astype(BF16)
    w_lo = (w - w_hi.astype(F32)).astype(BF16)
    a = jnp.dot(hl_scr[...], w_hi, preferred_element_type=F32)
    o = a[:m, :] + a[m:, :] + jnp.dot(hl_scr[:m, :], w_lo, preferred_element_type=F32)
    o_ref[...] = o + r_ref[...] if has_res else o


def _hp_matmul(x, w, widx, n_out, norm_gain=None, res=None, out_buf=None, out_row0=0):
    m, k = x.shape
    tn = _pick(n_out, (512, 256, 128))
    norm = norm_gain is not None
    gain = (norm_gain if norm else jnp.ones((k,), F32)).reshape(1, k)
    in_specs = [pl.BlockSpec((m, k), lambda j: (0, 0)), pl.BlockSpec((1, k), lambda j: (0, 0)),
                pl.BlockSpec((None, k, tn), lambda j: (widx, 0, j))]
    args = [x, gain, w]
    if res is not None:
        in_specs.append(pl.BlockSpec((m, tn), lambda j: (0, j)))
        args.append(res)
    aliases = {}
    out_shape = jax.ShapeDtypeStruct((m, n_out), F32)
    rb = 0
    if out_buf is not None:
        in_specs.append(pl.BlockSpec(memory_space=pl.ANY))
        aliases = {len(args): 0}
        args.append(out_buf)
        out_shape = jax.ShapeDtypeStruct(out_buf.shape, F32)
        rb = out_row0 // m
    return pl.pallas_call(
        functools.partial(_hp_mm_kernel, norm=norm, has_res=res is not None, m=m),
        grid=(n_out // tn,), in_specs=in_specs,
        out_specs=pl.BlockSpec((m, tn), lambda j: (rb, j)),
        out_shape=out_shape, scratch_shapes=[pltpu.VMEM((2 * m, k), BF16)],
        input_output_aliases=aliases,
        compiler_params=_cp(("arbitrary",), 48), name="proj_sample",
    )(*args)


def _gla_gate_kernel(h_ref, wad_ref, wa2_ref, ba_ref, g_ref):
    ad = jnp.dot(h_ref[...], wad_ref[...], preferred_element_type=F32)
    x = jnp.dot(ad.astype(BF16), wa2_ref[...], preferred_element_type=F32) + ba_ref[...]
    log_sig = jnp.minimum(x, 0.0) - jnp.log1p(jnp.exp(-jnp.abs(x)))
    g_ref[...] = log_sig / D_GATE_TAU


def _gla_gate(h, wad, wa2, ba):
    t, d = h.shape
    tm = _pick(t, (640, 512, 256, 128))
    return pl.pallas_call(
        _gla_gate_kernel, grid=(t // tm,),
        in_specs=[pl.BlockSpec((tm, d), lambda i: (i, 0)), pl.BlockSpec((d, LANES), lambda i: (0, 0)),
                  pl.BlockSpec((LANES, D_QK), lambda i: (0, 0)), pl.BlockSpec((1, D_QK), lambda i: (0, 0))],
        out_specs=pl.BlockSpec((tm, D_QK), lambda i: (i, 0)),
        out_shape=jax.ShapeDtypeStruct((t, D_QK), F32),
        compiler_params=_cp(("parallel",), 40), name="gla_gate",
    )(h, wad, wa2, ba)


def _cumsum_rows(g):
    c, n = g.shape
    row = lax.broadcasted_iota(jnp.int32, (c, c), 0)
    col = lax.broadcasted_iota(jnp.int32, (c, c), 1)
    tri = jnp.where(row >= col, 1.0, 0.0).astype(BF16)
    hi, mid, lo = _split3(g)
    cat = jnp.concatenate([hi, mid, lo], axis=1).astype(BF16)
    r = jnp.dot(tri, cat, preferred_element_type=F32)
    return r[:, :n] + r[:, n:2 * n] + r[:, 2 * n:]


def _col_bcast(rows, i):
    m, n = rows.shape
    hi, mid, lo = _split3(rows)
    lhs = jnp.concatenate([hi, mid, lo], axis=0).astype(BF16)
    r = lax.broadcasted_iota(jnp.int32, (3 * m, LANES), 0)
    sel = jnp.where((r == i) | (r == m + i) | (r == 2 * m + i), 1.0, 0.0).astype(BF16)
    return lax.dot_general(lhs, sel, _TN, preferred_element_type=F32)


def _chunk_step(q, k, v, g, s_scr, ones_k):
    c, dk = q.shape
    dv = v.shape[1]
    rep = dv // LANES
    G = _cumsum_rows(g)
    g_last = G[c - 1:c, :]
    vb = v.astype(BF16)
    s = s_scr[...]
    o = jnp.dot((q * jnp.exp(G)).astype(BF16), s.astype(BF16), preferred_element_type=F32)
    kd = (k * jnp.exp(g_last - G)).astype(BF16)
    upd = lax.dot_general(kd, vb, _TN, preferred_element_type=F32)
    pad = jnp.concatenate([jnp.exp(g_last), jnp.zeros((BF16_ROWS - 1, dk), F32)], axis=0)
    s_scr[...] = s * _tile_lanes(_col_bcast(pad, 0), rep) + upd
    parts = []
    for i in range(c):
        nj = SUBLANES * (i // SUBLANES + 1)
        dec = jnp.exp(jnp.minimum(G[i:i + 1, :] - G[:nj, :], 0.0))
        parts.append(dec * (q[i:i + 1, :] * k[:nj, :]))
    att = jnp.dot(jnp.concatenate(parts, axis=0).astype(BF16), ones_k, preferred_element_type=F32)
    sub = lax.broadcasted_iota(jnp.int32, (SUBLANES, LANES), 0)
    rows = []
    off = 0
    for i in range(c):
        nj = SUBLANES * (i // SUBLANES + 1)
        last = jnp.where(sub <= (i % SUBLANES), att[off + nj - SUBLANES:off + nj, :], 0.0)
        a_i = last if nj == SUBLANES else jnp.concatenate([att[off:off + nj - SUBLANES, :], last], axis=0)
        off += nj
        rows.append(jnp.sum(_tile_lanes(a_i, rep) * v[:nj, :], axis=0, keepdims=True))
    return o + jnp.concatenate(rows, axis=0)


def _head_rms(o, gain):
    return o * lax.rsqrt(jnp.mean(o * o, axis=-1, keepdims=True) + EPS) * gain


def _group_norm(o, gain):
    mu = jnp.mean(o, axis=-1, keepdims=True)
    var = jnp.mean(jnp.square(o - mu), axis=-1, keepdims=True)
    return (o - mu) * lax.rsqrt(var + EPS) * gain


def _hgrn_gates(f, lb):
    forget = lb + (1.0 - lb) * jax.nn.sigmoid(f)
    return forget, (1.0 - lb) * jax.nn.sigmoid(-f)


def _hgrn_prompt_kernel(q_ref, f_ref, i_ref, og_ref, lb_ref, gain_ref, o_ref, s_ref, s_scr, *, nsub):
    @pl.when(pl.program_id(2) == 0)
    def _():
        s_scr[...] = jnp.zeros_like(s_scr)

    lb = lb_ref[...]
    gain = gain_ref[...]
    ones_k = jnp.ones((A_DK, LANES), BF16)

    def body(sidx, carry):
        r = pl.ds(pl.multiple_of(sidx * REC_CHUNK, REC_CHUNK), REC_CHUNK)
        forget, k = _hgrn_gates(f_ref[r, :], lb)
        q = q_ref[r, :] * (A_DK ** -0.5)
        o = _chunk_step(q, k, i_ref[r, :], jnp.log(forget), s_scr, ones_k)
        o_ref[r, :] = (_head_rms(o, gain) * jax.nn.sigmoid(og_ref[r, :])).astype(BF16)
        return carry

    lax.fori_loop(0, nsub, body, 0)
    s_ref[0, 0] = s_scr[...]


def _hgrn_prompt(z, lb, gain, bsz, seq, t_all):
    rows = _pick(seq, (512, 256, 128, 64, 32))
    nblk = seq // rows
    hd = A_HEADS

    def col(off):
        return lambda b, h, c: (b * nblk + c, off + h)

    return pl.pallas_call(
        functools.partial(_hgrn_prompt_kernel, nsub=rows // REC_CHUNK),
        grid=(bsz, hd, nblk),
        in_specs=[pl.BlockSpec((rows, A_DK), col(0)), pl.BlockSpec((rows, A_DK), col(hd)),
                  pl.BlockSpec((rows, A_DV), col(2 * hd)), pl.BlockSpec((rows, A_DV), col(3 * hd)),
                  pl.BlockSpec((1, A_DK), lambda b, h, c: (0, h)),
                  pl.BlockSpec((1, A_DV), lambda b, h, c: (0, 0))],
        out_specs=[pl.BlockSpec((rows, A_DV), lambda b, h, c: (b * nblk + c, h)),
                   pl.BlockSpec((1, 1, A_DK, A_DV), lambda b, h, c: (b, h, 0, 0))],
        out_shape=[jax.ShapeDtypeStruct((t_all, D_MODEL), BF16),
                   jax.ShapeDtypeStruct((bsz, hd, A_DK, A_DV), F32)],
        scratch_shapes=[pltpu.VMEM((A_DK, A_DV), F32)],
        compiler_params=_cp(("parallel", "parallel", "arbitrary"), 32), name="hgrn_prompt",
    )(z, z, z, z, lb, gain)


def _sample_token(i, q_b, k_b, v, dec, gam_row, s_in_ref, s_out_ref):
    m, dv = v.shape
    rep = dv // LANES
    r = lax.broadcasted_iota(jnp.int32, (m, dv), 0)
    v_i = jnp.where(r == i, v, 0.0).astype(BF16)
    kv = lax.dot_general(k_b, v_i, _TN, preferred_element_type=F32)
    s0 = s_in_ref[i, 0]
    if dec is not None:
        s = s0 * _tile_lanes(_col_bcast(dec, i), rep) + kv
    else:
        s = s0 * gam_row + kv
    s_out_ref[i, 0] = s
    return jnp.dot(q_b, s.astype(BF16), preferred_element_type=F32)[i:i + 1, :]


def _split2(x):
    hi = x.astype(BF16).astype(F32)
    return hi, (x - hi).astype(BF16).astype(F32)


def _sample_token_hp(i, q2, k3, v_hi, v_lo, dec, s_in_ref, s_out_ref):
    m, dv = v_hi.shape
    rep = dv // LANES
    keep = lax.broadcasted_iota(jnp.int32, (m, dv), 0) == i
    vh = jnp.where(keep, v_hi, 0.0)
    vl = jnp.where(keep, v_lo, 0.0)
    rhs = jnp.concatenate([vh, vl, vh], axis=0).astype(BF16)
    kv = lax.dot_general(k3, rhs, _TN, preferred_element_type=F32)
    s = s_in_ref[i, 0] * _tile_lanes(_col_bcast(dec, i), rep) + kv
    s_out_ref[i, 0] = s
    s_hi = s.astype(BF16)
    s_lo = (s - s_hi.astype(F32)).astype(BF16)
    a = jnp.dot(q2, s_hi, preferred_element_type=F32)
    b = jnp.dot(q2[:m, :], s_lo, preferred_element_type=F32)
    return a[i:i + 1, :] + a[m + i:m + i + 1, :] + b[i:i + 1, :]


def _hgrn_sample_kernel(q_ref, f_ref, i_ref, og_ref, lb_ref, gain_ref, s_in_ref, o_ref, s_out_ref):
    forget, k = _hgrn_gates(f_ref[...], lb_ref[...])
    q_hi, q_lo = _split2(q_ref[...] * (A_DK ** -0.5))
    k_hi, k_lo = _split2(k)
    v_hi, v_lo = _split2(i_ref[...])
    q2 = jnp.concatenate([q_hi, q_lo], axis=0).astype(BF16)
    k3 = jnp.concatenate([k_hi, k_hi, k_lo], axis=0).astype(BF16)
    rows = [_sample_token_hp(i, q2, k3, v_hi, v_lo, forget, s_in_ref, s_out_ref) for i in range(SAMPLE_BLOCK)]
    o = jnp.concatenate(rows, axis=0)
    o_ref[...] = _head_rms(o, gain_ref[...]) * jax.nn.sigmoid(og_ref[...])


def _hgrn_sample(z, lb, gain, state):
    ts = state.shape[0]
    hd = A_HEADS
    sb = SAMPLE_BLOCK

    def col(off):
        return lambda b, h: (b, off + h)

    return pl.pallas_call(
        _hgrn_sample_kernel, grid=(ts // sb, hd),
        in_specs=[pl.BlockSpec((sb, A_DK), col(0)), pl.BlockSpec((sb, A_DK), col(hd)),
                  pl.BlockSpec((sb, A_DV), col(2 * hd)), pl.BlockSpec((sb, A_DV), col(3 * hd)),
                  pl.BlockSpec((1, A_DK), lambda b, h: (0, h)), pl.BlockSpec((1, A_DV), lambda b, h: (0, 0)),
                  pl.BlockSpec((sb, 1, A_DK, A_DV), lambda b, h: (b, h, 0, 0))],
        out_specs=[pl.BlockSpec((sb, A_DV), lambda b, h: (b, h)),
                   pl.BlockSpec((sb, 1, A_DK, A_DV), lambda b, h: (b, h, 0, 0))],
        out_shape=[jax.ShapeDtypeStruct((ts, D_MODEL), F32), jax.ShapeDtypeStruct(state.shape, F32)],
        compiler_params=_cp(("parallel", "parallel"), 32), name="hgrn_sample",
    )(z, z, z, z, lb, gain, state)


def _gelu_layernorm(v, ln_g, ln_b):
    vg = _gelu(v)
    mu = jnp.mean(vg, axis=-1, keepdims=True)
    var = jnp.mean(jnp.square(vg - mu), axis=-1, keepdims=True)
    return (vg - mu) * lax.rsqrt(var + EPS) * ln_g + ln_b


def _gmlp_kernel(u_ref, v_ref, lng_ref, lnb_ref, wm_ref, bst_ref, mix_ref, o_ref, cv_ref):
    del mix_ref
    vn = _gelu_layernorm(v_ref[...], lng_ref[...], lnb_ref[...])
    cv_ref[0] = vn
    ug = _gelu(u_ref[...])
    vnb = vn.astype(BF16)
    for g in range(B_GROUPS):
        cs = slice(g * B_GDIM, (g + 1) * B_GDIM)
        mixed = jnp.dot(wm_ref[g], vnb[:, cs], preferred_element_type=F32) + bst_ref[:, g:g + 1]
        o_ref[:, cs] = (ug[:, cs] * mixed).astype(BF16)


def _gmlp(z, ln_g, ln_b, w_s, b_s, mix, bsz, seq):
    cpb = seq // B_CHUNK
    wm = (w_s * jnp.tril(jnp.ones((B_CHUNK, B_CHUNK), F32))).astype(BF16)
    ucol = 4 * HALF // HALF
    full = lambda c: (0, 0)
    return pl.pallas_call(
        _gmlp_kernel, grid=(bsz * cpb,),
        in_specs=[pl.BlockSpec((B_CHUNK, HALF), lambda c: (c, ucol)),
                  pl.BlockSpec((B_CHUNK, HALF), lambda c: (c, ucol + 1)),
                  pl.BlockSpec((1, HALF), full), pl.BlockSpec((1, HALF), full),
                  pl.BlockSpec((B_GROUPS, B_CHUNK, B_CHUNK), lambda c: (0, 0, 0)),
                  pl.BlockSpec((B_CHUNK, B_GROUPS), full),
                  pl.BlockSpec(memory_space=pl.ANY)],
        out_specs=[pl.BlockSpec((B_CHUNK, HALF), lambda c: (c, 1)),
                   pl.BlockSpec((1, B_CHUNK, HALF), lambda c: (c // cpb, 0, 0))],
        out_shape=[jax.ShapeDtypeStruct(mix.shape, BF16),
                   jax.ShapeDtypeStruct((bsz, B_CHUNK, HALF), F32)],
        input_output_aliases={6: 0},
        compiler_params=_cp(("arbitrary",), 32), name="gmlp",
    )(z, z, ln_g.reshape(1, HALF), ln_b.reshape(1, HALF), wm, b_s.T, mix)


def _gmlp_sample_kernel(u_ref, v_ref, lng_ref, lnb_ref, d0_ref, c0_ref, mix_ref, o_ref, cv_ref):
    del mix_ref
    vn = _gelu_layernorm(v_ref[...], lng_ref[...], lnb_ref[...])
    cv_ref[...] = vn
    o_ref[...] = _gelu(u_ref[...]) * (vn * d0_ref[...] + c0_ref[...])


def _gmlp_sample(z, ln_g, ln_b, w_s, b_s, mix):
    ts = z.shape[0]
    tm = _pick(ts, (128, 64, 32, 16, 8))
    d0 = jnp.repeat(w_s[:, 0, 0], B_GDIM).reshape(1, HALF)
    c0 = jnp.repeat(b_s[:, 0], B_GDIM).reshape(1, HALF)
    ucol = 4 * HALF // HALF
    full = lambda c: (0, 0)
    return pl.pallas_call(
        _gmlp_sample_kernel, grid=(ts // tm,),
        in_specs=[pl.BlockSpec((tm, HALF), lambda c: (c, ucol)), pl.BlockSpec((tm, HALF), lambda c: (c, ucol + 1)),
                  pl.BlockSpec((1, HALF), full), pl.BlockSpec((1, HALF), full),
                  pl.BlockSpec((1, HALF), full), pl.BlockSpec((1, HALF), full),
                  pl.BlockSpec(memory_space=pl.ANY)],
        out_specs=[pl.BlockSpec((tm, HALF), lambda c: (c, 1)), pl.BlockSpec((tm, HALF), lambda c: (c, 0))],
        out_shape=[jax.ShapeDtypeStruct(mix.shape, F32), jax.ShapeDtypeStruct((ts, HALF), F32)],
        input_output_aliases={6: 0},
        compiler_params=_cp(("parallel",), 32), name="gmlp_sample",
    )(z, z, ln_g.reshape(1, HALF), ln_b.reshape(1, HALF), d0, c0, mix)


def _rotary(t, cos, sin):
    half = t.shape[1] // 2
    t1, t2 = t[:, :half], t[:, half:]
    return jnp.concatenate([t1 * cos - t2 * sin, t2 * cos + t1 * sin], axis=1)


def _ret_prompt_kernel(lg_ref, q_ref, k_ref, v_ref, gate_ref, cos_ref, sin_ref, gain_ref,
                       o_ref, s_ref, s_scr):
    @pl.when(pl.program_id(2) == 0)
    def _():
        s_scr[...] = jnp.zeros_like(s_scr)

    c = RET_CHUNK
    lg = lg_ref[pl.program_id(1)]
    cos, sin = cos_ref[...], sin_ref[...]
    qr = _rotary(q_ref[...], cos, sin)
    kr = _rotary(k_ref[...], cos, sin) * (C_DK ** -0.5)
    row = lax.broadcasted_iota(jnp.int32, (c, c), 0)
    col = lax.broadcasted_iota(jnp.int32, (c, c), 1)
    dec = jnp.where(row >= col, jnp.exp(lg * jnp.maximum(row - col, 0).astype(F32)), 0.0)
    ri = lax.broadcasted_iota(jnp.int32, (c, 1), 0).astype(F32)
    vb = v_ref[...].astype(BF16)
    s = s_scr[...]
    att = lax.dot_general(qr.astype(BF16), kr.astype(BF16), _NT, preferred_element_type=F32) * dec
    o = jnp.dot((qr * jnp.exp(lg * (ri + 1.0))).astype(BF16), s.astype(BF16), preferred_element_type=F32)
    o = o + jnp.dot(att.astype(BF16), vb, preferred_element_type=F32)
    kd = (kr * jnp.exp(lg * (c - 1.0 - ri))).astype(BF16)
    s_new = s * jnp.exp(lg * jnp.full((1, C_DV), float(c), F32)) + lax.dot_general(
        kd, vb, _TN, preferred_element_type=F32)
    s_scr[...] = s_new
    s_ref[0, 0] = s_new
    o_ref[...] = (_group_norm(o, gain_ref[...]) * _silu(gate_ref[...])).astype(BF16)


def _ret_prompt(z, log_gamma, cos, sin, gain, bsz, seq, t_all):
    c = RET_CHUNK
    nblk = seq // c
    hd = C_HEADS

    def col(off):
        return lambda b, h, j, lg: (b * nblk + j, off + h)

    tab = lambda b, h, j, lg: (j, 0)
    grid_spec = pltpu.PrefetchScalarGridSpec(
        num_scalar_prefetch=1, grid=(bsz, hd, nblk),
        in_specs=[pl.BlockSpec((c, C_DK), col(0)), pl.BlockSpec((c, C_DK), col(hd)),
                  pl.BlockSpec((c, C_DV), col(2 * hd)), pl.BlockSpec((c, C_DV), col(3 * hd)),
                  pl.BlockSpec((c, C_DK // 2), tab), pl.BlockSpec((c, C_DK // 2), tab),
                  pl.BlockSpec((1, C_DV), lambda b, h, j, lg: (0, h))],
        out_specs=[pl.BlockSpec((c, C_DV), lambda b, h, j, lg: (b * nblk + j, h)),
                   pl.BlockSpec((1, 1, C_DK, C_DV), lambda b, h, j, lg: (b, h, 0, 0))],
        scratch_shapes=[pltpu.VMEM((C_DK, C_DV), F32)])
    return pl.pallas_call(
        _ret_prompt_kernel, grid_spec=grid_spec,
        out_shape=[jax.ShapeDtypeStruct((t_all, D_MODEL), BF16),
                   jax.ShapeDtypeStruct((bsz, hd, C_DK, C_DV), F32)],
        compiler_params=_cp(("parallel", "parallel", "arbitrary"), 32), name="ret_prompt",
    )(log_gamma, z, z, z, z, cos, sin, gain)


def _ret_sample_kernel(lg_ref, q_ref, k_ref, v_ref, gate_ref, cos_ref, sin_ref, gain_ref, s_in_ref, mix_ref,
                       o_ref, s_out_ref):
    del mix_ref
    lg = lg_ref[pl.program_id(1)]
    cos, sin = cos_ref[...], sin_ref[...]
    q_b = _rotary(q_ref[...], cos, sin).astype(BF16)
    k_b = (_rotary(k_ref[...], cos, sin) * (C_DK ** -0.5)).astype(BF16)
    v = v_ref[...]
    gam_row = jnp.exp(lg * jnp.ones((1, C_DV), F32))
    rows = [_sample_token(i, q_b, k_b, v, None, gam_row, s_in_ref, s_out_ref) for i in range(SAMPLE_BLOCK)]
    o = jnp.concatenate(rows, axis=0)
    o_ref[...] = (_group_norm(o, gain_ref[...]) * _silu(gate_ref[...])).astype(BF16)


def _ret_sample(z, log_gamma, cos, sin, gain, state, mix, t_prompt):
    ts = state.shape[0]
    hd = C_HEADS
    base = t_prompt // SAMPLE_BLOCK

    def col(off):
        return lambda b, h, lg: (base + b, off + h)

    tab = lambda b, h, lg: (0, 0)
    grid_spec = pltpu.PrefetchScalarGridSpec(
        num_scalar_prefetch=1, grid=(ts // SAMPLE_BLOCK, hd),
        in_specs=[pl.BlockSpec((SAMPLE_BLOCK, C_DK), col(0)), pl.BlockSpec((SAMPLE_BLOCK, C_DK), col(hd)),
                  pl.BlockSpec((SAMPLE_BLOCK, C_DV), col(2 * hd)), pl.BlockSpec((SAMPLE_BLOCK, C_DV), col(3 * hd)),
                  pl.BlockSpec((1, C_DK // 2), tab), pl.BlockSpec((1, C_DK // 2), tab),
                  pl.BlockSpec((1, C_DV), lambda b, h, lg: (0, h)),
                  pl.BlockSpec((SAMPLE_BLOCK, 1, C_DK, C_DV), lambda b, h, lg: (b, h, 0, 0)),
                  pl.BlockSpec(memory_space=pl.ANY)],
        out_specs=[pl.BlockSpec((SAMPLE_BLOCK, C_DV), lambda b, h, lg: (base + b, h)),
                   pl.BlockSpec((SAMPLE_BLOCK, 1, C_DK, C_DV), lambda b, h, lg: (b, h, 0, 0))])
    return pl.pallas_call(
        _ret_sample_kernel, grid_spec=grid_spec,
        out_shape=[jax.ShapeDtypeStruct(mix.shape, BF16), jax.ShapeDtypeStruct(state.shape, F32)],
        input_output_aliases={9: 0},
        compiler_params=_cp(("parallel", "parallel"), 40), name="ret_sample",
    )(log_gamma, z, z, z, z, cos, sin, gain, state, mix)


def _gla_prompt_kernel(q_ref, k_ref, v_ref, r_ref, g_ref, gain_ref, mix_ref, o_ref, s_ref, s_scr, *, nsub):
    del mix_ref

    @pl.when(pl.program_id(2) == 0)
    def _():
        s_scr[...] = jnp.zeros_like(s_scr)

    gain = gain_ref[...]
    ones_k = jnp.ones((D_DK, LANES), BF16)

    def body(sidx, carry):
        r = pl.ds(pl.multiple_of(sidx * REC_CHUNK, REC_CHUNK), REC_CHUNK)
        q = q_ref[r, :] * (D_DK ** -0.5)
        o = _chunk_step(q, k_ref[r, :], v_ref[r, :], g_ref[r, :], s_scr, ones_k)
        o_ref[r, :] = (_head_rms(o, gain) * _silu(r_ref[r, :])).astype(BF16)
        return carry

    lax.fori_loop(0, nsub, body, 0)
    s_ref[0, 0] = s_scr[...]


def _gla_prompt(z, gate, gain, mix, bsz, seq):
    rows = _pick(seq, (256, 128, 64, 32))
    nblk = seq // rows
    hd = D_HEADS
    qoff = 4 * HALF // D_DK
    voff = (4 * HALF + 2 * D_QK) // D_DV
    rowcol = lambda off: (lambda b, h, c: (b * nblk + c, off + h))
    return pl.pallas_call(
        functools.partial(_gla_prompt_kernel, nsub=rows // REC_CHUNK),
        grid=(bsz, hd, nblk),
        in_specs=[pl.BlockSpec((rows, D_DK), rowcol(qoff)), pl.BlockSpec((rows, D_DK), rowcol(qoff + hd)),
                  pl.BlockSpec((rows, D_DV), rowcol(voff)), pl.BlockSpec((rows, D_DV), rowcol(voff + hd)),
                  pl.BlockSpec((rows, D_DK), rowcol(0)),
                  pl.BlockSpec((1, D_DV), lambda b, h, c: (0, 0)),
                  pl.BlockSpec(memory_space=pl.ANY)],
        out_specs=[pl.BlockSpec((rows, D_DV), rowcol(HALF // D_DV)),
                   pl.BlockSpec((1, 1, D_DK, D_DV), lambda b, h, c: (b, h, 0, 0))],
        out_shape=[jax.ShapeDtypeStruct(mix.shape, BF16),
                   jax.ShapeDtypeStruct((bsz, hd, D_DK, D_DV), F32)],
        scratch_shapes=[pltpu.VMEM((D_DK, D_DV), F32)],
        input_output_aliases={6: 0},
        compiler_params=_cp(("parallel", "parallel", "arbitrary"), 40), name="gla_prompt",
    )(z, z, z, z, gate, gain, mix)


def _gla_sample_kernel(q_ref, k_ref, v_ref, r_ref, g_ref, gain_ref, s_in_ref, mix_ref, o_ref, s_out_ref):
    del mix_ref
    q_b = (q_ref[...] * (D_DK ** -0.5)).astype(BF16)
    k_b = k_ref[...].astype(BF16)
    v = v_ref[...]
    dec = jnp.exp(g_ref[...])
    rows = [_sample_token(i, q_b, k_b, v, dec, None, s_in_ref, s_out_ref) for i in range(SAMPLE_BLOCK)]
    o = jnp.concatenate(rows, axis=0)
    o_ref[...] = (_head_rms(o, gain_ref[...]) * _silu(r_ref[...])).astype(BF16)


def _gla_sample(z, gate, gain, state, mix, t_prompt):
    ts = state.shape[0]
    hd = D_HEADS
    base = t_prompt // SAMPLE_BLOCK
    qoff = 4 * HALF // D_DK
    voff = (4 * HALF + 2 * D_QK) // D_DV
    rowcol = lambda off: (lambda b, h: (base + b, off + h))
    sb = SAMPLE_BLOCK
    return pl.pallas_call(
        _gla_sample_kernel, grid=(ts // sb, hd),
        in_specs=[pl.BlockSpec((sb, D_DK), rowcol(qoff)), pl.BlockSpec((sb, D_DK), rowcol(qoff + hd)),
                  pl.BlockSpec((sb, D_DV), rowcol(voff)), pl.BlockSpec((sb, D_DV), rowcol(voff + hd)),
                  pl.BlockSpec((sb, D_DK), rowcol(0)),
                  pl.BlockSpec((1, D_DV), lambda b, h: (0, 0)),
                  pl.BlockSpec((sb, 1, D_DK, D_DV), lambda b, h: (b, h, 0, 0)),
                  pl.BlockSpec(memory_space=pl.ANY)],
        out_specs=[pl.BlockSpec((sb, D_DV), rowcol(HALF // D_DV)),
                   pl.BlockSpec((sb, 1, D_DK, D_DV), lambda b, h: (b, h, 0, 0))],
        out_shape=[jax.ShapeDtypeStruct(mix.shape, BF16), jax.ShapeDtypeStruct(state.shape, F32)],
        input_output_aliases={7: 0},
        compiler_params=_cp(("parallel", "parallel"), 52), name="gla_sample",
    )(z, z, z, z, gate, gain, state, mix)


def _moe_kernel(te_ref, na_ref, x_ref, gate_ref, w1_ref, w3_ref, w2_ref, o_ref):
    del te_ref
    active = pl.program_id(0) < na_ref[0]

    @pl.when(active)
    def _():
        x = x_ref[...]
        a = jnp.dot(x, w1_ref[...], preferred_element_type=F32)
        b = jnp.dot(x, w3_ref[...], preferred_element_type=F32)
        hid = (_silu(a) * b * gate_ref[...]).astype(BF16)
        o_ref[...] = jnp.dot(hid, w2_ref[...], preferred_element_type=F32)

    @pl.when(jnp.logical_not(active))
    def _():
        o_ref[...] = jnp.zeros_like(o_ref)


def _moe_experts(xs, gate_rows, tile_expert, n_active, w1, w3, w2):
    p, d = xs.shape
    n_tiles = p // MOE_TILE
    wmap = lambda t, te, na: (te[t], 0, 0)
    grid_spec = pltpu.PrefetchScalarGridSpec(
        num_scalar_prefetch=2, grid=(n_tiles,),
        in_specs=[pl.BlockSpec((MOE_TILE, d), lambda t, te, na: (t, 0)),
                  pl.BlockSpec((MOE_TILE, 1), lambda t, te, na: (t, 0)),
                  pl.BlockSpec((None, d, D_EXPERT), wmap), pl.BlockSpec((None, d, D_EXPERT), wmap),
                  pl.BlockSpec((None, D_EXPERT, d), wmap)],
        out_specs=pl.BlockSpec((MOE_TILE, d), lambda t, te, na: (t, 0)))
    return pl.pallas_call(
        _moe_kernel, grid_spec=grid_spec,
        out_shape=jax.ShapeDtypeStruct((p, d), F32),
        compiler_params=_cp(("arbitrary",), 48), name="moe_experts",
    )(tile_expert, n_active, xs, gate_rows, w1, w3, w2)


def _hier_moe(x, ln_g, wg, bg, we, be, w1, w3, w2):
    t, d = x.shape
    n_router = N_GROUPS + N_EXPERTS
    wr = jnp.concatenate([wg, we.transpose(1, 0, 2).reshape(d, N_EXPERTS),
                          jnp.zeros((d, LANES - n_router), F32)], axis=1)
    h, logits = _rmsnorm_router(x, ln_g, wr)
    lg = logits[:, :N_GROUPS] + bg
    p_group = jnp.max(jax.nn.softmax(lg, axis=-1), axis=-1, keepdims=True)
    g_idx = jnp.argmax(lg, axis=-1)
    le = logits[:, N_GROUPS:n_router].reshape(t, N_GROUPS, EXPERTS_PER_GROUP) + be
    le_sel = jnp.take_along_axis(le, g_idx[:, None, None], axis=1)[:, 0, :]
    e_val, e_idx = lax.top_k(le_sel, TOP_K)
    w_top = jax.nn.softmax(e_val, axis=-1) * p_group
    expert = (g_idx[:, None] * EXPERTS_PER_GROUP + e_idx).astype(jnp.int32)
    flat_e = expert.reshape(-1)
    flat_w = w_top.reshape(-1)
    flat_tok = jnp.repeat(jnp.arange(t, dtype=jnp.int32), TOP_K)
    n_flat = t * TOP_K
    n_tiles = n_flat // MOE_TILE + N_EXPERTS
    p = n_tiles * MOE_TILE
    order = jnp.argsort(flat_e, stable=True)
    counts = jnp.zeros((N_EXPERTS,), jnp.int32).at[flat_e].add(1)
    padded = ((counts + MOE_TILE - 1) // MOE_TILE) * MOE_TILE
    pad_end = jnp.cumsum(padded)
    pad_start = pad_end - padded
    start = jnp.cumsum(counts) - counts
    sorted_e = flat_e[order]
    dest = pad_start[sorted_e] + (jnp.arange(n_flat, dtype=jnp.int32) - start[sorted_e])
    row_token = jnp.zeros((p,), jnp.int32).at[dest].set(flat_tok[order])
    row_gate = jnp.zeros((p,), F32).at[dest].set(flat_w[order])
    pos = jnp.zeros((n_flat,), jnp.int32).at[order].set(dest).reshape(t, TOP_K)
    n_active = (pad_end[-1] // MOE_TILE).astype(jnp.int32)
    tile_id = jnp.minimum(jnp.arange(n_tiles, dtype=jnp.int32), n_active - 1)
    tile_expert = jnp.minimum(jnp.searchsorted(pad_end, tile_id * MOE_TILE, side="right"),
                              N_EXPERTS - 1).astype(jnp.int32)
    xs = jnp.take(h, row_token, axis=0)
    ys = _moe_experts(xs, row_gate.reshape(p, 1), tile_expert, n_active.reshape(1),
                      w1.astype(BF16), w3.astype(BF16), w2.astype(BF16))
    return x + jnp.take(ys, pos[:, 0], axis=0) + jnp.take(ys, pos[:, 1], axis=0)


def kernel(x_prompt, x_sample, state_hgrn, state_ret, state_gla, ln_mix, ln_ffn, ln_final, w_in_ab, w_out_ab, lb_logits, hgrn_norm, gmlp_ln_g, gmlp_ln_b, gmlp_w_s, gmlp_b_s, w_in_cd, w_out_cd, ret_norm, gla_w_a2, gla_b_a, gla_norm, router_group_w, router_group_b, router_expert_w, router_expert_b, moe_w1, moe_w3, moe_w2):
    bsz, seq, d = x_prompt.shape
    ts, dec_seq, _ = x_sample.shape
    assert dec_seq == 1 and d == D_MODEL
    assert seq % RET_CHUNK == 0 and ts % B_CHUNK == 0
    tp = bsz * seq
    t_all = tp + ts
    depth = ln_mix.shape[0]

    xp = x_prompt.reshape(tp, d)
    xs = x_sample.reshape(ts, d)
    x = None
    lower_bounds = jnp.cumsum(jax.nn.softmax(lb_logits.astype(F32), axis=0), axis=0)
    half = C_DK // 2
    inv = ROPE_BASE ** (-jnp.arange(half, dtype=F32) / half)
    ang_p = jnp.arange(seq, dtype=jnp.int32).astype(F32)[:, None] * inv[None, :]
    ang_s = (PAST_LEN + jnp.arange(1, dtype=jnp.int32)).astype(F32)[:, None] * inv[None, :]
    log_gamma = jnp.log1p(-jnp.exp2(-5.0 - jnp.arange(C_HEADS, dtype=F32)))

    hg_p, hg_s, cv_p, cv_s, rt_p, rt_s, gl_p, gl_s = [], [], [], [], [], [], [], []
    for l in range(depth):
        j = l // 2
        if l % 2 == 0:
            assert l == 0, "prompt and sample rows enter layer 0 as separate arrays"
            lb = lower_bounds[j].reshape(1, HALF)
            gain = hgrn_norm[j].reshape(1, A_DV)
            z = _matmul(_rmsnorm(xp, ln_mix[l], BF16), w_in_ab, j, IN_AB)
            mix, s_p = _hgrn_prompt(z, lb, gain, bsz, seq, tp)
            mix, v_p = _gmlp(z, gmlp_ln_g[j], gmlp_ln_b[j], gmlp_w_s[j], gmlp_b_s[j], mix, bsz, seq)
            x = _matmul(mix, w_out_ab, j, D_MODEL, res=xp, out_rows=t_all)
            zs = _hp_matmul(xs, w_in_ab, j, IN_AB, norm_gain=ln_mix[l])
            mix_s, s_s = _hgrn_sample(zs, lb, gain, state_hgrn[j])
            mix_s, v_s = _gmlp_sample(zs, gmlp_ln_g[j], gmlp_ln_b[j], gmlp_w_s[j], gmlp_b_s[j], mix_s)
            x = _hp_matmul(mix_s, w_out_ab, j, D_MODEL, res=xs, out_buf=x, out_row0=tp)
            hg_p.append(s_p)
            hg_s.append(s_s)
            cv_p.append(v_p)
            cv_s.append(v_s.reshape(ts, 1, HALF))
        else:
            h = _rmsnorm(x, ln_mix[l], BF16)
            z = _matmul(h, w_in_cd, j, IN_CD_MAIN)
            wad = jnp.pad(w_in_cd[j, :, IN_CD_MAIN:], ((0, 0), (0, LANES - D_GATE_RANK))).astype(BF16)
            wa2 = jnp.pad(gla_w_a2[j], ((0, LANES - D_GATE_RANK), (0, 0))).astype(BF16)
            gate = _gla_gate(h, wad, wa2, gla_b_a[j].reshape(1, D_QK))
            rgain = ret_norm[j].reshape(1, HALF)
            ggain = gla_norm[j].reshape(1, D_DV)
            mix, s_cp = _ret_prompt(z, log_gamma, jnp.cos(ang_p), jnp.sin(ang_p), rgain, bsz, seq, t_all)
            mix, s_dp = _gla_prompt(z, gate, ggain, mix, bsz, seq)
            mix, s_cs = _ret_sample(z, log_gamma, jnp.cos(ang_s), jnp.sin(ang_s), rgain, state_ret[j], mix, tp)
            mix, s_ds = _gla_sample(z, gate, ggain, state_gla[j], mix, tp)
            rt_p.append(s_cp)
            rt_s.append(s_cs)
            gl_p.append(s_dp)
            gl_s.append(s_ds)
            x = _matmul(mix, w_out_cd, j, D_MODEL, res=x)
        x = _hier_moe(x, ln_ffn[l], router_group_w[l], router_group_b[l], router_expert_w[l],
                      router_expert_b[l], moe_w1[l], moe_w3[l], moe_w2[l])

    y = _rmsnorm(x, ln_final, F32)
    return (y[:tp].reshape(bsz, seq, d), y[tp:].reshape(ts, 1, d),
            jnp.stack(hg_p), jnp.stack(hg_s), jnp.stack(cv_p), jnp.stack(cv_s),
            jnp.stack(rt_p), jnp.stack(rt_s), jnp.stack(gl_p), jnp.stack(gl_s))
```

```python
import functools

import jax
import jax.numpy as jnp
from jax import lax
from jax.experimental import pallas as pl
from jax.experimental.pallas import tpu as pltpu

F32 = jnp.float32
BF16 = jnp.bfloat16

D_MODEL = 4096
HALF = D_MODEL // 2
PAST_LEN = 16384
A_DK, A_DV, A_HEADS = 128, 128, 16
B_CHUNK, B_GROUPS, B_GDIM = 128, 16, 128
C_HEADS, C_DK, C_DV = 8, 256, 256
ROPE_BASE = 10000.0
D_HEADS, D_QK, D_DK, D_DV = 4, 1024, 256, 512
D_GATE_RANK, D_GATE_TAU = 16, 16.0
IN_AB = 6 * HALF
IN_CD_MAIN = 4 * HALF + 2 * D_QK + 2 * HALF
N_GROUPS, EXPERTS_PER_GROUP, N_EXPERTS, TOP_K, D_EXPERT = 4, 8, 32, 2, 512
EPS = 1e-6

LANES = 128
SUBLANES = 8
BF16_ROWS = 16
REC_CHUNK = 32
A_HEAD_BLOCK = 4
RET_CHUNK = 256
SAMPLE_BLOCK = 16
MOE_TILE = 256

_TN = (((0,), (0,)), ((), ()))
_NT = (((1,), (1,)), ((), ()))


def _cp(sem, vmem_mb):
    return pltpu.CompilerParams(dimension_semantics=sem, vmem_limit_bytes=vmem_mb << 20)


def _pick(n, candidates):
    for c in candidates:
        if n % c == 0:
            return c
    raise ValueError(f"no block size for {n}")


def _tile_lanes(x, rep):
    return x if rep == 1 else jnp.concatenate([x] * rep, axis=1)


def _split3(x):
    hi = x.astype(BF16).astype(F32)
    r = x - hi
    mid = r.astype(BF16).astype(F32)
    return hi, mid, r - mid


def _gelu(x):
    return 0.5 * x * (1.0 + lax.erf(x * 0.7071067811865476))


def _silu(x):
    return x * jax.nn.sigmoid(x)


def _norm_kernel(x_ref, g_ref, o_ref):
    x = x_ref[...]
    r = lax.rsqrt(jnp.mean(x * x, axis=-1, keepdims=True) + EPS)
    o_ref[...] = (x * r * g_ref[...]).astype(o_ref.dtype)


def _rmsnorm(x, g, out_dtype):
    t, d = x.shape
    tm = _pick(t, (320, 256, 128, 64, 32, 16))
    return pl.pallas_call(
        _norm_kernel, grid=(t // tm,),
        in_specs=[pl.BlockSpec((tm, d), lambda i: (i, 0)), pl.BlockSpec((1, d), lambda i: (0, 0))],
        out_specs=pl.BlockSpec((tm, d), lambda i: (i, 0)),
        out_shape=jax.ShapeDtypeStruct((t, d), out_dtype),
        compiler_params=_cp(("parallel",), 40), name="rmsnorm",
    )(x, g.reshape(1, d))


def _norm_split_kernel(x_ref, g_ref, op_ref, os_ref, *, n_prompt):
    x = x_ref[...]
    y = x * lax.rsqrt(jnp.mean(x * x, axis=-1, keepdims=True) + EPS) * g_ref[...]
    is_prompt = pl.program_id(0) < n_prompt

    @pl.when(is_prompt)
    def _():
        op_ref[...] = y

    @pl.when(jnp.logical_not(is_prompt))
    def _():
        os_ref[...] = y


def _rmsnorm_split(x, g, tp):
    t, d = x.shape
    tm = B_CHUNK
    n_prompt = tp // tm
    return pl.pallas_call(
        functools.partial(_norm_split_kernel, n_prompt=n_prompt), grid=(t // tm,),
        in_specs=[pl.BlockSpec((tm, d), lambda i: (i, 0)), pl.BlockSpec((1, d), lambda i: (0, 0))],
        out_specs=[pl.BlockSpec((tm, d), lambda i: (jnp.minimum(i, n_prompt - 1), 0)),
                   pl.BlockSpec((tm, d), lambda i: (jnp.maximum(i - n_prompt, 0), 0))],
        out_shape=[jax.ShapeDtypeStruct((tp, d), F32), jax.ShapeDtypeStruct((t - tp, d), F32)],
        compiler_params=_cp(("arbitrary",), 32), name="rmsnorm_final",
    )(x, g.reshape(1, d))


def _norm_router_kernel(x_ref, g_ref, whi_ref, wlo_ref, h_ref, lg_ref):
    x = x_ref[...]
    r = lax.rsqrt(jnp.mean(x * x, axis=-1, keepdims=True) + EPS)
    hf = x * r * g_ref[...]
    h = hf.astype(BF16)
    h_ref[...] = h
    lo = (hf - h.astype(F32)).astype(BF16)
    whi = whi_ref[...]
    lg_ref[...] = (jnp.dot(h, whi, preferred_element_type=F32) + jnp.dot(lo, whi, preferred_element_type=F32)
                   + jnp.dot(h, wlo_ref[...], preferred_element_type=F32))


def _rmsnorm_router(x, g, wr):
    t, d = x.shape
    tm = _pick(t, (320, 256, 128, 64, 32, 16))
    whi = wr.astype(BF16)
    wlo = (wr - whi.astype(F32)).astype(BF16)
    return pl.pallas_call(
        _norm_router_kernel, grid=(t // tm,),
        in_specs=[pl.BlockSpec((tm, d), lambda i: (i, 0)), pl.BlockSpec((1, d), lambda i: (0, 0)),
                  pl.BlockSpec((d, LANES), lambda i: (0, 0)), pl.BlockSpec((d, LANES), lambda i: (0, 0))],
        out_specs=[pl.BlockSpec((tm, d), lambda i: (i, 0)), pl.BlockSpec((tm, LANES), lambda i: (i, 0))],
        out_shape=[jax.ShapeDtypeStruct((t, d), BF16), jax.ShapeDtypeStruct((t, LANES), F32)],
        compiler_params=_cp(("parallel",), 40), name="rmsnorm_router",
    )(x, g.reshape(1, d), whi, wlo)


def _mm_kernel(h_ref, w_ref, o_ref):
    o_ref[...] = jnp.dot(h_ref[...], w_ref[...].astype(BF16), preferred_element_type=F32)


def _mm_res_kernel(h_ref, w_ref, r_ref, o_ref):
    o_ref[...] = r_ref[...] + jnp.dot(h_ref[...], w_ref[...].astype(BF16), preferred_element_type=F32)


def _matmul(h, w, widx, n_out, res=None, out_rows=None):
    t, k = h.shape
    tm = _pick(t, (2048, 1664, 1280, 1024, 640, 512, 256, 128))
    tn = _pick(n_out, (256, 128) if tm > 1664 else (512, 256, 128))
    in_specs = [pl.BlockSpec((tm, k), lambda i, j: (i, 0), pipeline_mode=pl.Buffered(1)),
                pl.BlockSpec((None, k, tn), lambda i, j: (widx, 0, j))]
    args = [h, w]
    kern = _mm_kernel
    if res is not None:
        in_specs.append(pl.BlockSpec((tm, tn), lambda i, j: (i, j)))
        args.append(res)
        kern = _mm_res_kernel
    return pl.pallas_call(
        kern, grid=(t // tm, n_out // tn), in_specs=in_specs,
        out_specs=pl.BlockSpec((tm, tn), lambda i, j: (i, j)),
        out_shape=jax.ShapeDtypeStruct((out_rows or t, n_out), F32),
        compiler_params=_cp(("parallel", "arbitrary"), 56), name="proj",
    )(*args)


def _hp_mm_kernel(*refs, norm, has_res, m):
    x_ref, g_ref, w_ref = refs[:3]
    r_ref = refs[3] if has_res else None
    o_ref, hl_scr = refs[-2], refs[-1]

    @pl.when(pl.program_id(0) == 0)
    def _():
        x = x_ref[...]
        if norm:
            x = x * lax.rsqrt(jnp.mean(x * x, axis=-1, keepdims=True) + EPS) * g_ref[...]
        hi = x.astype(BF16)
        hl_scr[:m, :] = hi
        hl_scr[m:, :] = (x - hi.astype(F32)).astype(BF16)

    w = w_ref[...]
    w_hi = w.astype(BF16)
    w_lo = (w - w_hi.astype(F32)).astype(BF16)
    a = jnp.dot(hl_scr[...], w_hi, preferred_element_type=F32)
    o = a[:m, :] + a[m:, :] + jnp.dot(hl_scr[:m, :], w_lo, preferred_element_type=F32)
    o_ref[...] = o + r_ref[...] if has_res else o


def _hp_matmul(x, w, widx, n_out, norm_gain=None, res=None, out_buf=None, out_row0=0):
    m, k = x.shape
    tn = _pick(n_out, (512, 256, 128))
    norm = norm_gain is not None
    gain = (norm_gain if norm else jnp.ones((k,), F32)).reshape(1, k)
    in_specs = [pl.BlockSpec((m, k), lambda j: (0, 0)), pl.BlockSpec((1, k), lambda j: (0, 0)),
                pl.BlockSpec((None, k, tn), lambda j: (widx, 0, j))]
    args = [x, gain, w]
    if res is not None:
        in_specs.append(pl.BlockSpec((m, tn), lambda j: (0, j)))
        args.append(res)
    aliases = {}
    out_shape = jax.ShapeDtypeStruct((m, n_out), F32)
    rb = 0
    if out_buf is not None:
        in_specs.append(pl.BlockSpec(memory_space=pl.ANY))
        aliases = {len(args): 0}
        args.append(out_buf)
        out_shape = jax.ShapeDtypeStruct(out_buf.shape, F32)
        rb = out_row0 // m
    return pl.pallas_call(
        functools.partial(_hp_mm_kernel, norm=norm, has_res=res is not None, m=m),
        grid=(n_out // tn,), in_specs=in_specs,
        out_specs=pl.BlockSpec((m, tn), lambda j: (rb, j)),
        out_shape=out_shape, scratch_shapes=[pltpu.VMEM((2 * m, k), BF16)],
        input_output_aliases=aliases,
        compiler_params=_cp(("arbitrary",), 48), name="proj_sample",
    )(*args)


def _gla_gate_kernel(h_ref, wad_ref, wa2_ref, ba_ref, g_ref):
    ad = jnp.dot(h_ref[...], wad_ref[...], preferred_element_type=F32)
    x = jnp.dot(ad.astype(BF16), wa2_ref[...], preferred_element_type=F32) + ba_ref[...]
    log_sig = jnp.minimum(x, 0.0) - jnp.log1p(jnp.exp(-jnp.abs(x)))
    g_ref[...] = log_sig / D_GATE_TAU


def _gla_gate(h, wad, wa2, ba):
    t, d = h.shape
    tm = _pick(t, (640, 512, 256, 128))
    return pl.pallas_call(
        _gla_gate_kernel, grid=(t // tm,),
        in_specs=[pl.BlockSpec((tm, d), lambda i: (i, 0)), pl.BlockSpec((d, LANES), lambda i: (0, 0)),
                  pl.BlockSpec((LANES, D_QK), lambda i: (0, 0)), pl.BlockSpec((1, D_QK), lambda i: (0, 0))],
        out_specs=pl.BlockSpec((tm, D_QK), lambda i: (i, 0)),
        out_shape=jax.ShapeDtypeStruct((t, D_QK), F32),
        compiler_params=_cp(("parallel",), 40), name="gla_gate",
    )(h, wad, wa2, ba)


def _cumsum_rows(g):
    c, n = g.shape
    row = lax.broadcasted_iota(jnp.int32, (c, c), 0)
    col = lax.broadcasted_iota(jnp.int32, (c, c), 1)
    tri = jnp.where(row >= col, 1.0, 0.0).astype(BF16)
    hi, mid, lo = _split3(g)
    cat = jnp.concatenate([hi, mid, lo], axis=1).astype(BF16)
    r = jnp.dot(tri, cat, preferred_element_type=F32)
    return r[:, :n] + r[:, n:2 * n] + r[:, 2 * n:]


def _col_bcast(rows, i):
    m, n = rows.shape
    hi, mid, lo = _split3(rows)
    lhs = jnp.concatenate([hi, mid, lo], axis=0).astype(BF16)
    r = lax.broadcasted_iota(jnp.int32, (3 * m, LANES), 0)
    sel = jnp.where((r == i) | (r == m + i) | (r == 2 * m + i), 1.0, 0.0).astype(BF16)
    return lax.dot_general(lhs, sel, _TN, preferred_element_type=F32)


def _chunk_step(q, k, v, g, s_scr, ones_k):
    c, dk = q.shape
    dv = v.shape[1]
    rep = dv // LANES
    G = _cumsum_rows(g)
    g_last = G[c - 1:c, :]
    vb = v.astype(BF16)
    s = s_scr[...]
    o = jnp.dot((q * jnp.exp(G)).astype(BF16), s.astype(BF16), preferred_element_type=F32)
    kd = (k * jnp.exp(g_last - G)).astype(BF16)
    upd = lax.dot_general(kd, vb, _TN, preferred_element_type=F32)
    pad = jnp.concatenate([jnp.exp(g_last), jnp.zeros((BF16_ROWS - 1, dk), F32)], axis=0)
    s_scr[...] = s * _tile_lanes(_col_bcast(pad, 0), rep) + upd
    parts = []
    for i in range(c):
        nj = SUBLANES * (i // SUBLANES + 1)
        dec = jnp.exp(jnp.minimum(G[i:i + 1, :] - G[:nj, :], 0.0))
        parts.append(dec * (q[i:i + 1, :] * k[:nj, :]))
    att = jnp.dot(jnp.concatenate(parts, axis=0).astype(BF16), ones_k, preferred_element_type=F32)
    sub = lax.broadcasted_iota(jnp.int32, (SUBLANES, LANES), 0)
    rows = []
    off = 0
    for i in range(c):
        nj = SUBLANES * (i // SUBLANES + 1)
        last = jnp.where(sub <= (i % SUBLANES), att[off + nj - SUBLANES:off + nj, :], 0.0)
        a_i = last if nj == SUBLANES else jnp.concatenate([att[off:off + nj - SUBLANES, :], last], axis=0)
        off += nj
        rows.append(jnp.sum(_tile_lanes(a_i, rep) * v[:nj, :], axis=0, keepdims=True))
    return o + jnp.concatenate(rows, axis=0)


def _head_rms(o, gain):
    return o * lax.rsqrt(jnp.mean(o * o, axis=-1, keepdims=True) + EPS) * gain


def _group_norm(o, gain):
    mu = jnp.mean(o, axis=-1, keepdims=True)
    var = jnp.mean(jnp.square(o - mu), axis=-1, keepdims=True)
    return (o - mu) * lax.rsqrt(var + EPS) * gain


def _hgrn_gates(f, lb):
    forget = lb + (1.0 - lb) * jax.nn.sigmoid(f)
    return forget, (1.0 - lb) * jax.nn.sigmoid(-f)


def _hgrn_prompt_kernel(q_ref, f_ref, i_ref, og_ref, lb_ref, gain_ref, o_ref, s_ref, s_scr, *, nsub):
    @pl.when(pl.program_id(2) == 0)
    def _():
        s_scr[...] = jnp.zeros_like(s_scr)

    gain = gain_ref[...]
    ones_k = jnp.ones((A_DK, LANES), BF16)

    def body(sidx, carry):
        r = pl.ds(pl.multiple_of(sidx * REC_CHUNK, REC_CHUNK), REC_CHUNK)
        for hh in range(A_HEAD_BLOCK):
            cs = slice(hh * A_DK, (hh + 1) * A_DK)
            forget, k = _hgrn_gates(f_ref[r, cs], lb_ref[:, cs])
            q = q_ref[r, cs] * (A_DK ** -0.5)
            o = _chunk_step(q, k, i_ref[r, cs], jnp.log(forget), s_scr.at[hh], ones_k)
            o_ref[r, cs] = (_head_rms(o, gain) * jax.nn.sigmoid(og_ref[r, cs])).astype(BF16)
        return carry

    lax.fori_loop(0, nsub, body, 0)
    s_ref[0] = s_scr[...]


def _hgrn_prompt(z, lb, gain, bsz, seq, t_all):
    rows = _pick(seq, (512, 256, 128, 64, 32))
    nblk = seq // rows
    hb = A_HEAD_BLOCK
    nhb = A_HEADS // hb
    w = hb * A_DK

    def col(off):
        return lambda b, h, c: (b * nblk + c, off + h)

    return pl.pallas_call(
        functools.partial(_hgrn_prompt_kernel, nsub=rows // REC_CHUNK),
        grid=(bsz, nhb, nblk),
        in_specs=[pl.BlockSpec((rows, w), col(0)), pl.BlockSpec((rows, w), col(nhb)),
                  pl.BlockSpec((rows, w), col(2 * nhb)), pl.BlockSpec((rows, w), col(3 * nhb)),
                  pl.BlockSpec((1, w), lambda b, h, c: (0, h)),
                  pl.BlockSpec((1, A_DV), lambda b, h, c: (0, 0))],
        out_specs=[pl.BlockSpec((rows, w), lambda b, h, c: (b * nblk + c, h)),
                   pl.BlockSpec((1, hb, A_DK, A_DV), lambda b, h, c: (b, h, 0, 0))],
        out_shape=[jax.ShapeDtypeStruct((t_all, D_MODEL), BF16),
                   jax.ShapeDtypeStruct((bsz, A_HEADS, A_DK, A_DV), F32)],
        scratch_shapes=[pltpu.VMEM((hb, A_DK, A_DV), F32)],
        compiler_params=_cp(("parallel", "parallel", "arbitrary"), 40), name="hgrn_prompt",
    )(z, z, z, z, lb, gain)


def _sample_token(i, q_b, k_b, v, dec, gam_row, s_in_ref, s_out_ref):
    m, dv = v.shape
    rep = dv // LANES
    r = lax.broadcasted_iota(jnp.int32, (m, dv), 0)
    v_i = jnp.where(r == i, v, 0.0).astype(BF16)
    kv = lax.dot_general(k_b, v_i, _TN, preferred_element_type=F32)
    s0 = s_in_ref[i, 0]
    if dec is not None:
        s = s0 * _tile_lanes(_col_bcast(dec, i), rep) + kv
    else:
        s = s0 * gam_row + kv
    s_out_ref[i, 0] = s
    return jnp.dot(q_b, s.astype(BF16), preferred_element_type=F32)[i:i + 1, :]


def _split2(x):
    hi = x.astype(BF16).astype(F32)
    return hi, (x - hi).astype(BF16).astype(F32)


def _sample_token_hp(i, q2, k3, v_hi, v_lo, dec, s_in_ref, s_out_ref):
    m, dv = v_hi.shape
    rep = dv // LANES
    keep = lax.broadcasted_iota(jnp.int32, (m, dv), 0) == i
    vh = jnp.where(keep, v_hi, 0.0)
    vl = jnp.where(keep, v_lo, 0.0)
    rhs = jnp.concatenate([vh, vl, vh], axis=0).astype(BF16)
    kv = lax.dot_general(k3, rhs, _TN, preferred_element_type=F32)
    s = s_in_ref[i, 0] * _tile_lanes(_col_bcast(dec, i), rep) + kv
    s_out_ref[i, 0] = s
    s_hi = s.astype(BF16)
    s_lo = (s - s_hi.astype(F32)).astype(BF16)
    a = jnp.dot(q2, s_hi, preferred_element_type=F32)
    b = jnp.dot(q2[:m, :], s_lo, preferred_element_type=F32)
    return a[i:i + 1, :] + a[m + i:m + i + 1, :] + b[i:i + 1, :]


def _hgrn_sample_kernel(q_ref, f_ref, i_ref, og_ref, lb_ref, gain_ref, s_in_ref, o_ref, s_out_ref):
    forget, k = _hgrn_gates(f_ref[...], lb_ref[...])
    q_hi, q_lo = _split2(q_ref[...] * (A_DK ** -0.5))
    k_hi, k_lo = _split2(k)
    v_hi, v_lo = _split2(i_ref[...])
    q2 = jnp.concatenate([q_hi, q_lo], axis=0).astype(BF16)
    k3 = jnp.concatenate([k_hi, k_hi, k_lo], axis=0).astype(BF16)
    rows = [_sample_token_hp(i, q2, k3, v_hi, v_lo, forget, s_in_ref, s_out_ref) for i in range(SAMPLE_BLOCK)]
    o = jnp.concatenate(rows, axis=0)
    o_ref[...] = _head_rms(o, gain_ref[...]) * jax.nn.sigmoid(og_ref[...])


def _hgrn_sample(z, lb, gain, state):
    ts = state.shape[0]
    hd = A_HEADS
    sb = SAMPLE_BLOCK

    def col(off):
        return lambda b, h: (b, off + h)

    return pl.pallas_call(
        _hgrn_sample_kernel, grid=(ts // sb, hd),
        in_specs=[pl.BlockSpec((sb, A_DK), col(0)), pl.BlockSpec((sb, A_DK), col(hd)),
                  pl.BlockSpec((sb, A_DV), col(2 * hd)), pl.BlockSpec((sb, A_DV), col(3 * hd)),
                  pl.BlockSpec((1, A_DK), lambda b, h: (0, h)), pl.BlockSpec((1, A_DV), lambda b, h: (0, 0)),
                  pl.BlockSpec((sb, 1, A_DK, A_DV), lambda b, h: (b, h, 0, 0))],
        out_specs=[pl.BlockSpec((sb, A_DV), lambda b, h: (b, h)),
                   pl.BlockSpec((sb, 1, A_DK, A_DV), lambda b, h: (b, h, 0, 0))],
        out_shape=[jax.ShapeDtypeStruct((ts, D_MODEL), F32), jax.ShapeDtypeStruct(state.shape, F32)],
        compiler_params=_cp(("parallel", "parallel"), 32), name="hgrn_sample",
    )(z, z, z, z, lb, gain, state)


def _gelu_layernorm(v, ln_g, ln_b):
    vg = _gelu(v)
    mu = jnp.mean(vg, axis=-1, keepdims=True)
    var = jnp.mean(jnp.square(vg - mu), axis=-1, keepdims=True)
    return (vg - mu) * lax.rsqrt(var + EPS) * ln_g + ln_b


def _gmlp_kernel(u_ref, v_ref, lng_ref, lnb_ref, wm_ref, bst_ref, mix_ref, o_ref, cv_ref):
    del mix_ref
    vn = _gelu_layernorm(v_ref[...], lng_ref[...], lnb_ref[...])
    cv_ref[0] = vn
    ug = _gelu(u_ref[...])
    vnb = vn.astype(BF16)
    for g in range(B_GROUPS):
        cs = slice(g * B_GDIM, (g + 1) * B_GDIM)
        mixed = jnp.dot(wm_ref[g], vnb[:, cs], preferred_element_type=F32) + bst_ref[:, g:g + 1]
        o_ref[:, cs] = (ug[:, cs] * mixed).astype(BF16)


def _gmlp(z, ln_g, ln_b, w_s, b_s, mix, bsz, seq):
    cpb = seq // B_CHUNK
    wm = (w_s * jnp.tril(jnp.ones((B_CHUNK, B_CHUNK), F32))).astype(BF16)
    ucol = 4 * HALF // HALF
    full = lambda c: (0, 0)
    return pl.pallas_call(
        _gmlp_kernel, grid=(bsz * cpb,),
        in_specs=[pl.BlockSpec((B_CHUNK, HALF), lambda c: (c, ucol)),
                  pl.BlockSpec((B_CHUNK, HALF), lambda c: (c, ucol + 1)),
                  pl.BlockSpec((1, HALF), full), pl.BlockSpec((1, HALF), full),
                  pl.BlockSpec((B_GROUPS, B_CHUNK, B_CHUNK), lambda c: (0, 0, 0)),
                  pl.BlockSpec((B_CHUNK, B_GROUPS), full),
                  pl.BlockSpec(memory_space=pl.ANY)],
        out_specs=[pl.BlockSpec((B_CHUNK, HALF), lambda c: (c, 1)),
                   pl.BlockSpec((1, B_CHUNK, HALF), lambda c: (c // cpb, 0, 0))],
        out_shape=[jax.ShapeDtypeStruct(mix.shape, BF16),
                   jax.ShapeDtypeStruct((bsz, B_CHUNK, HALF), F32)],
        input_output_aliases={6: 0},
        compiler_params=_cp(("arbitrary",), 32), name="gmlp",
    )(z, z, ln_g.reshape(1, HALF), ln_b.reshape(1, HALF), wm, b_s.T, mix)


def _gmlp_sample_kernel(u_ref, v_ref, lng_ref, lnb_ref, d0_ref, c0_ref, mix_ref, o_ref, cv_ref):
    del mix_ref
    vn = _gelu_layernorm(v_ref[...], lng_ref[...], lnb_ref[...])
    cv_ref[...] = vn
    o_ref[...] = _gelu(u_ref[...]) * (vn * d0_ref[...] + c0_ref[...])


def _gmlp_sample(z, ln_g, ln_b, w_s, b_s, mix):
    ts = z.shape[0]
    tm = _pick(ts, (128, 64, 32, 16, 8))
    d0 = jnp.repeat(w_s[:, 0, 0], B_GDIM).reshape(1, HALF)
    c0 = jnp.repeat(b_s[:, 0], B_GDIM).reshape(1, HALF)
    ucol = 4 * HALF // HALF
    full = lambda c: (0, 0)
    return pl.pallas_call(
        _gmlp_sample_kernel, grid=(ts // tm,),
        in_specs=[pl.BlockSpec((tm, HALF), lambda c: (c, ucol)), pl.BlockSpec((tm, HALF), lambda c: (c, ucol + 1)),
                  pl.BlockSpec((1, HALF), full), pl.BlockSpec((1, HALF), full),
                  pl.BlockSpec((1, HALF), full), pl.BlockSpec((1, HALF), full),
                  pl.BlockSpec(memory_space=pl.ANY)],
        out_specs=[pl.BlockSpec((tm, HALF), lambda c: (c, 1)), pl.BlockSpec((tm, HALF), lambda c: (c, 0))],
        out_shape=[jax.ShapeDtypeStruct(mix.shape, F32), jax.ShapeDtypeStruct((ts, HALF), F32)],
        input_output_aliases={6: 0},
        compiler_params=_cp(("parallel",), 32), name="gmlp_sample",
    )(z, z, ln_g.reshape(1, HALF), ln_b.reshape(1, HALF), d0, c0, mix)


def _rotary(t, cos, sin):
    half = t.shape[1] // 2
    t1, t2 = t[:, :half], t[:, half:]
    return jnp.concatenate([t1 * cos - t2 * sin, t2 * cos + t1 * sin], axis=1)


def _ret_prompt_kernel(lg_ref, q_ref, k_ref, v_ref, gate_ref, cos_ref, sin_ref, gain_ref,
                       o_ref, s_ref, s_scr):
    @pl.when(pl.program_id(2) == 0)
    def _():
        s_scr[...] = jnp.zeros_like(s_scr)

    c = RET_CHUNK
    lg = lg_ref[pl.program_id(1)]
    cos, sin = cos_ref[...], sin_ref[...]
    qr = _rotary(q_ref[...], cos, sin)
    kr = _rotary(k_ref[...], cos, sin) * (C_DK ** -0.5)
    row = lax.broadcasted_iota(jnp.int32, (c, c), 0)
    col = lax.broadcasted_iota(jnp.int32, (c, c), 1)
    dec = jnp.where(row >= col, jnp.exp(lg * jnp.maximum(row - col, 0).astype(F32)), 0.0)
    ri = lax.broadcasted_iota(jnp.int32, (c, 1), 0).astype(F32)
    vb = v_ref[...].astype(BF16)
    s = s_scr[...]
    att = lax.dot_general(qr.astype(BF16), kr.astype(BF16), _NT, preferred_element_type=F32) * dec
    o = jnp.dot((qr * jnp.exp(lg * (ri + 1.0))).astype(BF16), s.astype(BF16), preferred_element_type=F32)
    o = o + jnp.dot(att.astype(BF16), vb, preferred_element_type=F32)
    kd = (kr * jnp.exp(lg * (c - 1.0 - ri))).astype(BF16)
    s_new = s * jnp.exp(lg * jnp.full((1, C_DV), float(c), F32)) + lax.dot_general(
        kd, vb, _TN, preferred_element_type=F32)
    s_scr[...] = s_new
    s_ref[0, 0] = s_new
    o_ref[...] = (_group_norm(o, gain_ref[...]) * _silu(gate_ref[...])).astype(BF16)


def _ret_prompt(z, log_gamma, cos, sin, gain, bsz, seq, t_all):
    c = RET_CHUNK
    nblk = seq // c
    hd = C_HEADS

    def col(off):
        return lambda b, h, j, lg: (b * nblk + j, off + h)

    tab = lambda b, h, j, lg: (j, 0)
    grid_spec = pltpu.PrefetchScalarGridSpec(
        num_scalar_prefetch=1, grid=(bsz, hd, nblk),
        in_specs=[pl.BlockSpec((c, C_DK), col(0)), pl.BlockSpec((c, C_DK), col(hd)),
                  pl.BlockSpec((c, C_DV), col(2 * hd)), pl.BlockSpec((c, C_DV), col(3 * hd)),
                  pl.BlockSpec((c, C_DK // 2), tab), pl.BlockSpec((c, C_DK // 2), tab),
                  pl.BlockSpec((1, C_DV), lambda b, h, j, lg: (0, h))],
        out_specs=[pl.BlockSpec((c, C_DV), lambda b, h, j, lg: (b * nblk + j, h)),
                   pl.BlockSpec((1, 1, C_DK, C_DV), lambda b, h, j, lg: (b, h, 0, 0))],
        scratch_shapes=[pltpu.VMEM((C_DK, C_DV), F32)])
    return pl.pallas_call(
        _ret_prompt_kernel, grid_spec=grid_spec,
        out_shape=[jax.ShapeDtypeStruct((t_all, D_MODEL), BF16),
                   jax.ShapeDtypeStruct((bsz, hd, C_DK, C_DV), F32)],
        compiler_params=_cp(("parallel", "parallel", "arbitrary"), 32), name="ret_prompt",
    )(log_gamma, z, z, z, z, cos, sin, gain)


def _ret_sample_kernel(lg_ref, q_ref, k_ref, v_ref, gate_ref, cos_ref, sin_ref, gain_ref, s_in_ref, mix_ref,
                       o_ref, s_out_ref):
    del mix_ref
    lg = lg_ref[pl.program_id(1)]
    cos, sin = cos_ref[...], sin_ref[...]
    q_b = _rotary(q_ref[...], cos, sin).astype(BF16)
    k_b = (_rotary(k_ref[...], cos, sin) * (C_DK ** -0.5)).astype(BF16)
    v = v_ref[...]
    gam_row = jnp.exp(lg * jnp.ones((1, C_DV), F32))
    rows = [_sample_token(i, q_b, k_b, v, None, gam_row, s_in_ref, s_out_ref) for i in range(SAMPLE_BLOCK)]
    o = jnp.concatenate(rows, axis=0)
    o_ref[...] = (_group_norm(o, gain_ref[...]) * _silu(gate_ref[...])).astype(BF16)


def _ret_sample(z, log_gamma, cos, sin, gain, state, mix, t_prompt):
    ts = state.shape[0]
    hd = C_HEADS
    base = t_prompt // SAMPLE_BLOCK

    def col(off):
        return lambda b, h, lg: (base + b, off + h)

    tab = lambda b, h, lg: (0, 0)
    grid_spec = pltpu.PrefetchScalarGridSpec(
        num_scalar_prefetch=1, grid=(ts // SAMPLE_BLOCK, hd),
        in_specs=[pl.BlockSpec((SAMPLE_BLOCK, C_DK), col(0)), pl.BlockSpec((SAMPLE_BLOCK, C_DK), col(hd)),
                  pl.BlockSpec((SAMPLE_BLOCK, C_DV), col(2 * hd)), pl.BlockSpec((SAMPLE_BLOCK, C_DV), col(3 * hd)),
                  pl.BlockSpec((1, C_DK // 2), tab), pl.BlockSpec((1, C_DK // 2), tab),
                  pl.BlockSpec((1, C_DV), lambda b, h, lg: (0, h)),
                  pl.BlockSpec((SAMPLE_BLOCK, 1, C_DK, C_DV), lambda b, h, lg: (b, h, 0, 0)),
                  pl.BlockSpec(memory_space=pl.ANY)],
        out_specs=[pl.BlockSpec((SAMPLE_BLOCK, C_DV), lambda b, h, lg: (base + b, h)),
                   pl.BlockSpec((SAMPLE_BLOCK, 1, C_DK, C_DV), lambda b, h, lg: (b, h, 0, 0))])
    return pl.pallas_call(
        _ret_sample_kernel, grid_spec=grid_spec,
        out_shape=[jax.ShapeDtypeStruct(mix.shape, BF16), jax.ShapeDtypeStruct(state.shape, F32)],
        input_output_aliases={9: 0},
        compiler_params=_cp(("parallel", "parallel"), 40), name="ret_sample",
    )(log_gamma, z, z, z, z, cos, sin, gain, state, mix)


def _gla_prompt_kernel(q_ref, k_ref, v_ref, r_ref, g_ref, gain_ref, mix_ref, o_ref, s_ref, s_scr, *, nsub):
    del mix_ref

    @pl.when(pl.program_id(1) == 0)
    def _():
        s_scr[...] = jnp.zeros_like(s_scr)

    gain = gain_ref[...]
    ones_k = jnp.ones((D_DK, LANES), BF16)

    def body(sidx, carry):
        r = pl.ds(pl.multiple_of(sidx * REC_CHUNK, REC_CHUNK), REC_CHUNK)
        for hh in range(D_HEADS):
            ks = slice(hh * D_DK, (hh + 1) * D_DK)
            vs = slice(hh * D_DV, (hh + 1) * D_DV)
            q = q_ref[r, ks] * (D_DK ** -0.5)
            o = _chunk_step(q, k_ref[r, ks], v_ref[r, vs], g_ref[r, ks], s_scr.at[hh], ones_k)
            o_ref[r, vs] = (_head_rms(o, gain) * _silu(r_ref[r, vs])).astype(BF16)
        return carry

    lax.fori_loop(0, nsub, body, 0)
    s_ref[0] = s_scr[...]


def _gla_prompt(z, gate, gain, mix, bsz, seq):
    rows = _pick(seq, (256, 128, 64, 32))
    nblk = seq // rows
    qoff = 4 * HALF // D_QK
    voff = (4 * HALF + 2 * D_QK) // HALF
    rowcol = lambda off: (lambda b, c: (b * nblk + c, off))
    return pl.pallas_call(
        functools.partial(_gla_prompt_kernel, nsub=rows // REC_CHUNK),
        grid=(bsz, nblk),
        in_specs=[pl.BlockSpec((rows, D_QK), rowcol(qoff)), pl.BlockSpec((rows, D_QK), rowcol(qoff + 1)),
                  pl.BlockSpec((rows, HALF), rowcol(voff)), pl.BlockSpec((rows, HALF), rowcol(voff + 1)),
                  pl.BlockSpec((rows, D_QK), rowcol(0)),
                  pl.BlockSpec((1, D_DV), lambda b, c: (0, 0)),
                  pl.BlockSpec(memory_space=pl.ANY)],
        out_specs=[pl.BlockSpec((rows, HALF), rowcol(1)),
                   pl.BlockSpec((1, D_HEADS, D_DK, D_DV), lambda b, c: (b, 0, 0, 0))],
        out_shape=[jax.ShapeDtypeStruct(mix.shape, BF16),
                   jax.ShapeDtypeStruct((bsz, D_HEADS, D_DK, D_DV), F32)],
        scratch_shapes=[pltpu.VMEM((D_HEADS, D_DK, D_DV), F32)],
        input_output_aliases={6: 0},
        compiler_params=_cp(("parallel", "arbitrary"), 48), name="gla_prompt",
    )(z, z, z, z, gate, gain, mix)


def _gla_sample_kernel(q_ref, k_ref, v_ref, r_ref, g_ref, gain_ref, s_in_ref, mix_ref, o_ref, s_out_ref):
    del mix_ref
    q_b = (q_ref[...] * (D_DK ** -0.5)).astype(BF16)
    k_b = k_ref[...].astype(BF16)
    v = v_ref[...]
    dec = jnp.exp(g_ref[...])
    rows = [_sample_token(i, q_b, k_b, v, dec, None, s_in_ref, s_out_ref) for i in range(SAMPLE_BLOCK)]
    o = jnp.concatenate(rows, axis=0)
    o_ref[...] = (_head_rms(o, gain_ref[...]) * _silu(r_ref[...])).astype(BF16)


def _gla_sample(z, gate, gain, state, mix, t_prompt):
    ts = state.shape[0]
    hd = D_HEADS
    base = t_prompt // SAMPLE_BLOCK
    qoff = 4 * HALF // D_DK
    voff = (4 * HALF + 2 * D_QK) // D_DV
    rowcol = lambda off: (lambda b, h: (base + b, off + h))
    sb = SAMPLE_BLOCK
    return pl.pallas_call(
        _gla_sample_kernel, grid=(ts // sb, hd),
        in_specs=[pl.BlockSpec((sb, D_DK), rowcol(qoff)), pl.BlockSpec((sb, D_DK), rowcol(qoff + hd)),
                  pl.BlockSpec((sb, D_DV), rowcol(voff)), pl.BlockSpec((sb, D_DV), rowcol(voff + hd)),
                  pl.BlockSpec((sb, D_DK), rowcol(0)),
                  pl.BlockSpec((1, D_DV), lambda b, h: (0, 0)),
                  pl.BlockSpec((sb, 1, D_DK, D_DV), lambda b, h: (b, h, 0, 0)),
                  pl.BlockSpec(memory_space=pl.ANY)],
        out_specs=[pl.BlockSpec((sb, D_DV), rowcol(HALF // D_DV)),
                   pl.BlockSpec((sb, 1, D_DK, D_DV), lambda b, h: (b, h, 0, 0))],
        out_shape=[jax.ShapeDtypeStruct(mix.shape, BF16), jax.ShapeDtypeStruct(state.shape, F32)],
        input_output_aliases={7: 0},
        compiler_params=_cp(("parallel", "parallel"), 52), name="gla_sample",
    )(z, z, z, z, gate, gain, state, mix)


def _moe_up_kernel(te_ref, na_ref, x_ref, gate_ref, w1_ref, w3_ref, hid_ref, w1b, w3b):
    t = pl.program_id(1)
    fresh = jnp.logical_or(t == 0, te_ref[t] != te_ref[jnp.maximum(t - 1, 0)])

    @pl.when(fresh)
    def _():
        w1b[...] = w1_ref[...].astype(BF16)
        w3b[...] = w3_ref[...].astype(BF16)

    @pl.when(t < na_ref[0])
    def _():
        x = x_ref[...]
        a = jnp.dot(x, w1b[...], preferred_element_type=F32)
        b = jnp.dot(x, w3b[...], preferred_element_type=F32)
        hid_ref[...] = (_silu(a) * b * gate_ref[...]).astype(BF16)

    @pl.when(t >= na_ref[0])
    def _():
        hid_ref[...] = jnp.zeros_like(hid_ref)


def _moe_down_kernel(te_ref, na_ref, hid_ref, w2_ref, o_ref, w2b):
    t = pl.program_id(0)
    fresh = jnp.logical_or(t == 0, te_ref[t] != te_ref[jnp.maximum(t - 1, 0)])

    @pl.when(fresh)
    def _():
        w2b[...] = w2_ref[...].astype(BF16)

    @pl.when(t < na_ref[0])
    def _():
        o_ref[...] = jnp.dot(hid_ref[...], w2b[...], preferred_element_type=F32)

    @pl.when(t >= na_ref[0])
    def _():
        o_ref[...] = jnp.zeros_like(o_ref)


def _moe_experts(xs, gate_rows, tile_expert, n_active, w1, w3, w2, layer):
    p, d = xs.shape
    n_tiles = p // MOE_TILE
    fh = D_EXPERT // 2
    up_spec = pltpu.PrefetchScalarGridSpec(
        num_scalar_prefetch=2, grid=(2, n_tiles),
        in_specs=[pl.BlockSpec((MOE_TILE, d), lambda f, t, te, na: (t, 0)),
                  pl.BlockSpec((MOE_TILE, 1), lambda f, t, te, na: (t, 0)),
                  pl.BlockSpec((None, None, d, fh), lambda f, t, te, na: (layer, te[t], 0, f)),
                  pl.BlockSpec((None, None, d, fh), lambda f, t, te, na: (layer, te[t], 0, f))],
        out_specs=pl.BlockSpec((MOE_TILE, fh), lambda f, t, te, na: (t, f)),
        scratch_shapes=[pltpu.VMEM((d, fh), BF16), pltpu.VMEM((d, fh), BF16)])
    hid = pl.pallas_call(
        _moe_up_kernel, grid_spec=up_spec,
        out_shape=jax.ShapeDtypeStruct((p, D_EXPERT), BF16),
        compiler_params=_cp(("arbitrary", "arbitrary"), 40), name="moe_up",
    )(tile_expert, n_active, xs, gate_rows, w1, w3)
    down_spec = pltpu.PrefetchScalarGridSpec(
        num_scalar_prefetch=2, grid=(n_tiles,),
        in_specs=[pl.BlockSpec((MOE_TILE, D_EXPERT), lambda t, te, na: (t, 0)),
                  pl.BlockSpec((None, None, D_EXPERT, d), lambda t, te, na: (layer, te[t], 0, 0))],
        out_specs=pl.BlockSpec((MOE_TILE, d), lambda t, te, na: (t, 0)),
        scratch_shapes=[pltpu.VMEM((D_EXPERT, d), BF16)])
    return pl.pallas_call(
        _moe_down_kernel, grid_spec=down_spec,
        out_shape=jax.ShapeDtypeStruct((p, d), F32),
        compiler_params=_cp(("arbitrary",), 40), name="moe_down",
    )(tile_expert, n_active, hid, w2)


def _hier_moe(x, ln_g, wg, bg, we, be, w1, w3, w2, layer):
    t, d = x.shape
    n_router = N_GROUPS + N_EXPERTS
    wr = jnp.concatenate([wg, we.transpose(1, 0, 2).reshape(d, N_EXPERTS),
                          jnp.zeros((d, LANES - n_router), F32)], axis=1)
    h, logits = _rmsnorm_router(x, ln_g, wr)
    lg = logits[:, :N_GROUPS] + bg
    p_group = jnp.max(jax.nn.softmax(lg, axis=-1), axis=-1, keepdims=True)
    g_idx = jnp.argmax(lg, axis=-1)
    le = logits[:, N_GROUPS:n_router].reshape(t, N_GROUPS, EXPERTS_PER_GROUP) + be
    le_sel = jnp.sum(le * jax.nn.one_hot(g_idx, N_GROUPS, dtype=F32)[:, :, None], axis=1)
    lane = jnp.arange(EXPERTS_PER_GROUP, dtype=jnp.int32)[None, :]
    i1 = jnp.argmax(le_sel, axis=-1).astype(jnp.int32)
    rest = jnp.where(lane == i1[:, None], -jnp.inf, le_sel)
    i2 = jnp.argmax(rest, axis=-1).astype(jnp.int32)
    e_val = jnp.stack([jnp.max(le_sel, axis=-1), jnp.max(rest, axis=-1)], axis=-1)
    w_top = jax.nn.softmax(e_val, axis=-1) * p_group
    expert = g_idx.astype(jnp.int32)[:, None] * EXPERTS_PER_GROUP + jnp.stack([i1, i2], axis=-1)
    flat_e = expert.reshape(-1)
    n_flat = t * TOP_K
    n_tiles = n_flat // MOE_TILE + N_EXPERTS
    p = n_tiles * MOE_TILE
    hot = (flat_e[:, None] == jnp.arange(N_EXPERTS, dtype=jnp.int32)[None, :]).astype(jnp.int32)
    before = jnp.cumsum(hot, axis=0) - hot
    counts = jnp.sum(hot, axis=0)
    padded = ((counts + MOE_TILE - 1) // MOE_TILE) * MOE_TILE
    pad_end = jnp.cumsum(padded)
    dest = jnp.sum((before + (pad_end - padded)[None, :]) * hot, axis=1)
    row_token = jnp.zeros((p,), jnp.int32).at[dest].set(jnp.repeat(jnp.arange(t, dtype=jnp.int32), TOP_K))
    row_gate = jnp.zeros((p,), F32).at[dest].set(w_top.reshape(-1))
    pos = dest.reshape(t, TOP_K)
    n_active = pad_end[-1] // MOE_TILE
    tile_start = jnp.minimum(jnp.arange(n_tiles, dtype=jnp.int32), n_active - 1) * MOE_TILE
    tile_expert = jnp.minimum(jnp.sum((pad_end[None, :] <= tile_start[:, None]).astype(jnp.int32), axis=1),
                              N_EXPERTS - 1)
    xs = jnp.take(h, row_token, axis=0, mode="clip")
    ys = _moe_experts(xs, row_gate.reshape(p, 1), tile_expert, n_active.reshape(1), w1, w3, w2, layer)
    return x + jnp.take(ys, pos[:, 0], axis=0, mode="clip") + jnp.take(ys, pos[:, 1], axis=0, mode="clip")


def kernel(x_prompt, x_sample, state_hgrn, state_ret, state_gla, ln_mix, ln_ffn, ln_final, w_in_ab, w_out_ab, lb_logits, hgrn_norm, gmlp_ln_g, gmlp_ln_b, gmlp_w_s, gmlp_b_s, w_in_cd, w_out_cd, ret_norm, gla_w_a2, gla_b_a, gla_norm, router_group_w, router_group_b, router_expert_w, router_expert_b, moe_w1, moe_w3, moe_w2):
    bsz, seq, d = x_prompt.shape
    ts, dec_seq, _ = x_sample.shape
    assert dec_seq == 1 and d == D_MODEL
    assert seq % RET_CHUNK == 0 and ts % B_CHUNK == 0
    tp = bsz * seq
    t_all = tp + ts
    depth = ln_mix.shape[0]

    xp = x_prompt.reshape(tp, d)
    xs = x_sample.reshape(ts, d)
    x = None
    lower_bounds = jnp.cumsum(jax.nn.softmax(lb_logits.astype(F32), axis=0), axis=0)
    half = C_DK // 2
    inv = ROPE_BASE ** (-jnp.arange(half, dtype=F32) / half)
    ang_p = jnp.arange(seq, dtype=jnp.int32).astype(F32)[:, None] * inv[None, :]
    ang_s = (PAST_LEN + jnp.arange(1, dtype=jnp.int32)).astype(F32)[:, None] * inv[None, :]
    log_gamma = jnp.log1p(-jnp.exp2(-5.0 - jnp.arange(C_HEADS, dtype=F32)))

    hg_p, hg_s, cv_p, cv_s, rt_p, rt_s, gl_p, gl_s = [], [], [], [], [], [], [], []
    for l in range(depth):
        j = l // 2
        if l % 2 == 0:
            assert l == 0, "prompt and sample rows enter layer 0 as separate arrays"
            lb = lower_bounds[j].reshape(1, HALF)
            gain = hgrn_norm[j].reshape(1, A_DV)
            z = _matmul(_rmsnorm(xp, ln_mix[l], BF16), w_in_ab, j, IN_AB)
            mix, s_p = _hgrn_prompt(z, lb, gain, bsz, seq, tp)
            mix, v_p = _gmlp(z, gmlp_ln_g[j], gmlp_ln_b[j], gmlp_w_s[j], gmlp_b_s[j], mix, bsz, seq)
            x = _matmul(mix, w_out_ab, j, D_MODEL, res=xp, out_rows=t_all)
            zs = _hp_matmul(xs, w_in_ab, j, IN_AB, norm_gain=ln_mix[l])
            mix_s, s_s = _hgrn_sample(zs, lb, gain, state_hgrn[j])
            mix_s, v_s = _gmlp_sample(zs, gmlp_ln_g[j], gmlp_ln_b[j], gmlp_w_s[j], gmlp_b_s[j], mix_s)
            x = _hp_matmul(mix_s, w_out_ab, j, D_MODEL, res=xs, out_buf=x, out_row0=tp)
            hg_p.append(s_p)
            hg_s.append(s_s)
            cv_p.append(v_p)
            cv_s.append(v_s.reshape(ts, 1, HALF))
        else:
            h = _rmsnorm(x, ln_mix[l], BF16)
            z = _matmul(h, w_in_cd, j, IN_CD_MAIN)
            wad = jnp.pad(w_in_cd[j, :, IN_CD_MAIN:], ((0, 0), (0, LANES - D_GATE_RANK))).astype(BF16)
            wa2 = jnp.pad(gla_w_a2[j], ((0, LANES - D_GATE_RANK), (0, 0))).astype(BF16)
            gate = _gla_gate(h, wad, wa2, gla_b_a[j].reshape(1, D_QK))
            rgain = ret_norm[j].reshape(1, HALF)
            ggain = gla_norm[j].reshape(1, D_DV)
            mix, s_cp = _ret_prompt(z, log_gamma, jnp.cos(ang_p), jnp.sin(ang_p), rgain, bsz, seq, t_all)
            mix, s_dp = _gla_prompt(z, gate, ggain, mix, bsz, seq)
            mix, s_cs = _ret_sample(z, log_gamma, jnp.cos(ang_s), jnp.sin(ang_s), rgain, state_ret[j], mix, tp)
            mix, s_ds = _gla_sample(z, gate, ggain, state_gla[j], mix, tp)
            rt_p.append(s_cp)
            rt_s.append(s_cs)
            gl_p.append(s_dp)
            gl_s.append(s_ds)
            x = _matmul(mix, w_out_cd, j, D_MODEL, res=x)
        x = _hier_moe(x, ln_ffn[l], router_group_w[l], router_group_b[l], router_expert_w[l],
                      router_expert_b[l], moe_w1, moe_w3, moe_w2, l)

    y_p, y_s = _rmsnorm_split(x, ln_final, tp)
    return (y_p.reshape(bsz, seq, d), y_s.reshape(ts, 1, d),
            jnp.stack(hg_p), jnp.stack(hg_s), jnp.stack(cv_p), jnp.stack(cv_s),
            jnp.stack(rt_p), jnp.stack(rt_s), jnp.stack(gl_p), jnp.stack(gl_s))
```
